```python
import math
import jax, jax.numpy as jnp
from jax import lax
import numpy as np

D_MODEL = 2048
BATCH = 2
SEQ = 16384
DEPTH = 1

EPS = 1e-6
F_GROUPS = 4
F_GROUP_DIM = D_MODEL // 8
F_WIDTH = F_GROUPS * F_GROUP_DIM
GLA_HEADS = 4
GLA_DK = D_MODEL // 16
GLA_DV = D_MODEL // 8
GLA_QK = GLA_HEADS * GLA_DK
GLA_V = GLA_HEADS * GLA_DV
GATE_RANK = 16
GATE_TAU = 16.0
GLA_CHUNK = 64
N_DIRS = 2
W_IN_COLS = F_WIDTH + 2 * GLA_QK + 2 * GLA_V + N_DIRS * GATE_RANK + 2 * D_MODEL
PEER_HEADS = 8
PEER_QDIM = 256
PEER_HALF = PEER_QDIM // 2
N_KEYS = 128
N_EXPERTS = N_KEYS * N_KEYS
PEER_TOPK = 16
PEER_TOKEN_BLOCK = 128

kernel_name = "hybrid_fnet_gla_peer_adaln_block"


def rms_norm(x, g):
    xf = x.astype(jnp.float32)
    y = xf * lax.rsqrt(jnp.mean(xf * xf, axis=-1, keepdims=True) + EPS)
    return (y * g.astype(jnp.float32)).astype(x.dtype)


def fourier_mix(z):
    B, S, _ = z.shape
    zf = z.reshape(B, S, F_GROUPS, F_GROUP_DIM).astype(jnp.float32)
    y = jnp.fft.fftn(zf, axes=(1, 3), norm="ortho").real
    return y.reshape(B, S, F_WIDTH).astype(z.dtype)


def gla_chunked(q, k, v, log_a):
    N, H, S, DK = q.shape
    DV = v.shape[-1]
    nc = S // GLA_CHUNK

    def to_chunks(t):
        t = t.reshape(N, H, nc, GLA_CHUNK, t.shape[-1])
        return jnp.moveaxis(t, 2, 0)

    mask = jnp.tril(jnp.ones((GLA_CHUNK, GLA_CHUNK), dtype=bool))[None, None, :, :, None]

    def step(state, inp):
        qc, kc, vc, ac = inp
        b = jnp.cumsum(ac, axis=-2)
        o_inter = jnp.einsum('nhcd,nhde->nhce', qc * jnp.exp(b), state)
        diff = b[:, :, :, None, :] - b[:, :, None, :, :]
        decay = jnp.exp(jnp.where(mask, diff, -jnp.inf))
        attn = jnp.einsum('nhid,nhjd,nhijd->nhij', qc, kc, decay)
        o_intra = jnp.einsum('nhij,nhje->nhie', attn, vc)
        b_last = b[:, :, -1:, :]
        k_dec = kc * jnp.exp(b_last - b)
        state = jnp.exp(b_last[:, :, 0, :])[..., None] * state + jnp.einsum('nhcd,nhce->nhde', k_dec, vc)
        return state, o_inter + o_intra

    state0 = jnp.zeros((N, H, DK, DV), jnp.float32)
    _, out = lax.scan(step, state0, (to_chunks(q), to_chunks(k), to_chunks(v), to_chunks(log_a)))
    return jnp.moveaxis(out, 0, 2).reshape(N, H, S, DV)


def gla_branch(q, k, v, r, a1, w_a2, b_a, norm_g):
    B, S, _ = q.shape
    dtype = q.dtype
    qh = q.reshape(B, S, GLA_HEADS, GLA_DK).astype(jnp.float32) * (GLA_DK ** -0.5)
    kh = k.reshape(B, S, GLA_HEADS, GLA_DK).astype(jnp.float32)
    vh = v.reshape(B, S, GLA_HEADS, GLA_DV).astype(jnp.float32)
    a1 = a1.reshape(B, S, N_DIRS, GATE_RANK)
    z = jnp.einsum('bsir,ire->ibse', a1, w_a2) + b_a[:, None, None, :]
    log_a = (jax.nn.log_sigmoid(z.astype(jnp.float32)) / GATE_TAU).reshape(N_DIRS, B, S, GLA_HEADS, GLA_DK)

    def both_dirs(t):
        t2 = jnp.stack([t, jnp.flip(t, axis=1)])
        return jnp.transpose(t2.reshape((N_DIRS * B,) + t.shape[1:]), (0, 2, 1, 3))

    la = jnp.stack([log_a[0], jnp.flip(log_a[1], axis=1)])
    la = jnp.transpose(la.reshape(N_DIRS * B, S, GLA_HEADS, GLA_DK), (0, 2, 1, 3))
    o = gla_chunked(both_dirs(qh), both_dirs(kh), both_dirs(vh), la)
    o = o.reshape(N_DIRS, B, GLA_HEADS, S, GLA_DV)
    o = o[0] + jnp.flip(o[1], axis=2)
    o = jnp.transpose(o, (0, 2, 1, 3))
    o = o * lax.rsqrt(jnp.mean(o * o, axis=-1, keepdims=True) + EPS) * norm_g.astype(jnp.float32)
    o = o * jax.nn.silu(r.reshape(B, S, GLA_HEADS, GLA_DV).astype(jnp.float32))
    return o.reshape(B, S, GLA_V).astype(dtype)


def peer_ffn(n2, w_q, keys, u, v):
    B, S, D = n2.shape
    q = (n2 @ w_q).reshape(B, S, PEER_HEADS, PEER_QDIM).astype(jnp.float32)
    s1 = jnp.einsum('bshd,hnd->bshn', q[..., :PEER_HALF], keys[:, 0].astype(jnp.float32))
    s2 = jnp.einsum('bshd,hnd->bshn', q[..., PEER_HALF:], keys[:, 1].astype(jnp.float32))
    v1, i1 = lax.top_k(s1, PEER_TOPK)
    v2, i2 = lax.top_k(s2, PEER_TOPK)
    cand = (v1[..., :, None] + v2[..., None, :]).reshape(B, S, PEER_HEADS, PEER_TOPK * PEER_TOPK)
    sv, pos = lax.top_k(cand, PEER_TOPK)
    e1 = jnp.take_along_axis(i1, pos // PEER_TOPK, axis=-1)
    e2 = jnp.take_along_axis(i2, pos % PEER_TOPK, axis=-1)
    idx = e1 * N_KEYS + e2
    g = jax.nn.softmax(sv, axis=-1)

    T = B * S
    nb = T // PEER_TOKEN_BLOCK
    xb = n2.reshape(nb, PEER_TOKEN_BLOCK, D)
    ib = idx.reshape(nb, PEER_TOKEN_BLOCK, PEER_HEADS * PEER_TOPK)
    gb = g.reshape(nb, PEER_TOKEN_BLOCK, PEER_HEADS * PEER_TOPK)

    def block(args):
        xt, it, gt = args
        U = jnp.take(u, it, axis=0)
        act = jax.nn.gelu(jnp.einsum('td,tkd->tk', xt, U).astype(jnp.float32), approximate=False)
        V = jnp.take(v, it, axis=0)
        return jnp.einsum('tk,tkd->td', (gt * act).astype(xt.dtype), V)

    out = lax.map(block, (xb, ib, gb))
    return out.reshape(B, S, D)


def setup_inputs(seed: int = 0) -> dict:
    key = jax.random.key(seed)
    ks = jax.random.split(key, 20)
    D = D_MODEL
    f32 = jnp.float32
    nrm = lambda k, shape, s: jax.random.normal(k, shape, f32) * s
    return {
        "x": nrm(ks[0], (BATCH, SEQ, D), 1.0),
        "c": nrm(ks[1], (BATCH, D), 1.0),
        "w_ada": nrm(ks[2], (DEPTH, D, 6 * D), D ** -0.5),
        "b_ada": nrm(ks[3], (DEPTH, 6 * D), 0.02),
        "norm1_g": 1.0 + nrm(ks[4], (DEPTH, D), 0.02),
        "w_in": nrm(ks[5], (DEPTH, D, W_IN_COLS), D ** -0.5),
        "w_fnet": nrm(ks[6], (DEPTH, F_WIDTH, D), F_WIDTH ** -0.5),
        "gla_w_a2": nrm(ks[7], (DEPTH, N_DIRS, GATE_RANK, GLA_QK), GATE_RANK ** -0.5),
        "gla_b_a": nrm(ks[8], (DEPTH, N_DIRS, GLA_QK), 0.1),
        "gla_norm_g": 1.0 + nrm(ks[9], (DEPTH, GLA_DV), 0.02),
        "w_gla": nrm(ks[10], (DEPTH, GLA_V, D), GLA_V ** -0.5),
        "w_out": nrm(ks[11], (DEPTH, D, D), D ** -0.5),
        "norm2_g": 1.0 + nrm(ks[12], (DEPTH, D), 0.02),
        "peer_w_q": nrm(ks[13], (DEPTH, D, PEER_HEADS * PEER_QDIM), D ** -0.5),
        "peer_keys": nrm(ks[14], (DEPTH, PEER_HEADS, 2, N_KEYS, PEER_HALF), PEER_HALF ** -0.5),
        "peer_u": nrm(ks[15], (DEPTH, N_EXPERTS, D), D ** -0.5),
        "peer_v": nrm(ks[16], (DEPTH, N_EXPERTS, D), 0.5),
        "final_norm_g": 1.0 + nrm(ks[17], (D,), 0.02),
    }


def reference(x, c, w_ada, b_ada, norm1_g, w_in, w_fnet, gla_w_a2, gla_b_a, gla_norm_g,
              w_gla, w_out, norm2_g, peer_w_q, peer_keys, peer_u, peer_v, final_norm_g):
    split_points = [F_WIDTH,
                    F_WIDTH + GLA_QK,
                    F_WIDTH + 2 * GLA_QK,
                    F_WIDTH + 2 * GLA_QK + GLA_V,
                    F_WIDTH + 2 * GLA_QK + 2 * GLA_V,
                    F_WIDTH + 2 * GLA_QK + 2 * GLA_V + N_DIRS * GATE_RANK]
    h = x
    for l in range(DEPTH):
        mod = jax.nn.silu(c) @ w_ada[l] + b_ada[l]
        sh1, sc1, g1, sh2, sc2, g2 = jnp.split(mod[:, None, :], 6, axis=-1)

        n = rms_norm(h, norm1_g[l]) * (1.0 + sc1) + sh1
        p = n @ w_in[l]
        z_f, q, k, v, r, a1, m = jnp.split(p, split_points, axis=-1)
        y_f = fourier_mix(z_f) @ w_fnet[l]
        y_g = gla_branch(q, k, v, r, a1, gla_w_a2[l], gla_b_a[l], gla_norm_g[l]) @ w_gla[l]
        gate_f, gate_g = jnp.split(jax.nn.sigmoid(m), 2, axis=-1)
        mixed = (gate_f * y_f + gate_g * y_g) @ w_out[l]
        h = h + g1 * mixed

        n2 = rms_norm(h, norm2_g[l]) * (1.0 + sc2) + sh2
        h = h + g2 * peer_ffn(n2, peer_w_q[l], peer_keys[l], peer_u[l], peer_v[l])
    return rms_norm(h, final_norm_g)
```

```python
import functools
import math

import numpy as np
import jax
import jax.numpy as jnp
from jax import lax
from jax.experimental import pallas as pl
from jax.experimental.pallas import tpu as pltpu

BF16 = jnp.bfloat16
F32 = jnp.float32

EPS = 1e-6
F_GROUPS = 4
F_GROUP_DIM = 256
F_WIDTH = 1024
GLA_HEADS = 4
GLA_DK = 128
GLA_DV = 256
GLA_QK = 512
GLA_V = 1024
GATE_RANK = 16
GATE_TAU = 16.0
N_DIRS = 2
PEER_HEADS = 8
PEER_HALF = 128
N_KEYS = 128
PEER_TOPK = 16

LANES = 128
GLA_CHUNK = 128
FFT_N2 = 128
VMEM_LIMIT = 56 * 1024 * 1024

_NT = (((1,), (1,)), ((), ()))
_TN = (((0,), (0,)), ((), ()))


def _cparams(sem):
    return pltpu.CompilerParams(dimension_semantics=sem, vmem_limit_bytes=VMEM_LIMIT)


def _dot(a, b):
    return jnp.dot(a, b, preferred_element_type=F32)


def _split_bf16(a):
    hi = a.astype(BF16)
    lo = (a - hi.astype(F32)).astype(BF16)
    return hi, lo


def _ada_kernel(c_ref, w_ref, b_ref, o_ref):
    s = jax.nn.silu(c_ref[...])
    o_ref[...] = _dot(s.astype(BF16), w_ref[...].astype(BF16)) + b_ref[...]


def _ada(c, w, b, tn=1536):
    bsz, d = c.shape
    n = w.shape[1]
    cp = jnp.zeros((8, d), F32).at[:bsz].set(c)
    out = pl.pallas_call(
        _ada_kernel,
        out_shape=jax.ShapeDtypeStruct((8, n), F32),
        grid=(n // tn,),
        in_specs=[pl.BlockSpec((8, d), lambda j: (0, 0)),
                  pl.BlockSpec((d, tn), lambda j: (0, j)),
                  pl.BlockSpec((1, tn), lambda j: (0, j))],
        out_specs=pl.BlockSpec((8, tn), lambda j: (0, j)),
        compiler_params=_cparams(("arbitrary",)),
        name="ada",
    )(cp, w, b.reshape(1, n))
    return out[:bsz]


def _inproj_kernel(x_ref, g_ref, sc_ref, sh_ref, w_ref, wa_ref, p_ref, a1_ref, n_scr):
    @pl.when(pl.program_id(1) == 0)
    def _():
        x = x_ref[...]
        ms = jnp.mean(x * x, axis=-1, keepdims=True)
        y = x * lax.rsqrt(ms + EPS) * g_ref[...]
        nb = (y * (1.0 + sc_ref[0]) + sh_ref[0]).astype(BF16)
        n_scr[...] = nb
        a1_ref[...] = _dot(nb, wa_ref[...])

    p_ref[...] = _dot(n_scr[...], w_ref[...]).astype(BF16)


def _inproj(x2, g, mod3, w, wa, seq, tm=1024, tn=1024):
    t, d = x2.shape
    n = w.shape[1]
    tm = min(tm, seq)
    bidx = lambda i: (i * tm) // seq
    return pl.pallas_call(
        _inproj_kernel,
        out_shape=(jax.ShapeDtypeStruct((t, n), BF16), jax.ShapeDtypeStruct((t, LANES), F32)),
        grid=(t // tm, n // tn),
        in_specs=[pl.BlockSpec((tm, d), lambda i, j: (i, 0)),
                  pl.BlockSpec((1, d), lambda i, j: (0, 0)),
                  pl.BlockSpec((1, 1, d), lambda i, j: (bidx(i) * 6 + 1, 0, 0)),
                  pl.BlockSpec((1, 1, d), lambda i, j: (bidx(i) * 6, 0, 0)),
                  pl.BlockSpec((d, tn), lambda i, j: (0, j)),
                  pl.BlockSpec((d, LANES), lambda i, j: (0, 0))],
        out_specs=(pl.BlockSpec((tm, tn), lambda i, j: (i, j)),
                   pl.BlockSpec((tm, LANES), lambda i, j: (i, 0))),
        scratch_shapes=[pltpu.VMEM((tm, d), BF16)],
        compiler_params=_cparams(("parallel", "arbitrary")),
        name="inproj",
    )(x2, g.reshape(1, d), mod3, mod3, w, wa)


def _chan_dft_kernel(z_ref, f_ref, w_ref):
    for g in range(F_GROUPS):
        lo, hi = g * F_GROUP_DIM, (g + 1) * F_GROUP_DIM
        r = _dot(z_ref[:, lo:hi], f_ref[...])
        w_ref[:, lo:hi] = r[:, :F_GROUP_DIM].astype(BF16)
        w_ref[:, F_WIDTH + lo:F_WIDTH + hi] = r[:, F_GROUP_DIM:].astype(BF16)


def _chan_dft(p, fc, tm=1024):
    t = p.shape[0]
    tm = min(tm, t)
    return pl.pallas_call(
        _chan_dft_kernel,
        out_shape=jax.ShapeDtypeStruct((t, 2 * F_WIDTH), BF16),
        grid=(t // tm,),
        in_specs=[pl.BlockSpec((tm, F_WIDTH), lambda i: (i, 0)),
                  pl.BlockSpec((F_GROUP_DIM, 2 * F_GROUP_DIM), lambda i: (0, 0))],
        out_specs=pl.BlockSpec((tm, 2 * F_WIDTH), lambda i: (i, 0)),
        compiler_params=_cparams(("parallel",)),
        name="chan_dft",
    )(p, fc)


def _fft_a_kernel(w_ref, t_ref, x_ref):
    w = w_ref[0]
    rot = jnp.concatenate([w[:, F_WIDTH:], -w[:, :F_WIDTH]], axis=1)
    rhs = jnp.concatenate([w, rot], axis=0)
    x_ref[0, 0] = _dot(t_ref[0], rhs).astype(BF16)


def _fft_c_kernel(x_ref, t_ref, y_ref):
    x = x_ref[0]
    rhs = jnp.concatenate([x[:, :F_WIDTH], x[:, F_WIDTH:]], axis=0)
    y_ref[0] = _dot(t_ref[...], rhs).astype(BF16)


def _seq_dft(wc, bsz, seq, ta, tc):
    n2 = FFT_N2
    n1 = seq // n2
    w3 = wc.reshape(bsz, n1, n2 * 2 * F_WIDTH)
    x1 = pl.pallas_call(
        _fft_a_kernel,
        out_shape=jax.ShapeDtypeStruct((bsz, n2, n1, 2 * F_WIDTH), BF16),
        grid=(bsz, n2),
        in_specs=[pl.BlockSpec((1, n1, 2 * F_WIDTH), lambda b, s: (b, 0, s)),
                  pl.BlockSpec((1, n1, 2 * n1), lambda b, s: (s, 0, 0))],
        out_specs=pl.BlockSpec((1, 1, n1, 2 * F_WIDTH), lambda b, s: (b, s, 0, 0)),
        compiler_params=_cparams(("parallel", "parallel")),
        name="fft_a",
    )(w3, ta)
    x3 = x1.reshape(bsz, n2, n1 * 2 * F_WIDTH)
    y = pl.pallas_call(
        _fft_c_kernel,
        out_shape=jax.ShapeDtypeStruct((bsz, n2, n1 * F_WIDTH), BF16),
        grid=(bsz, n1),
        in_specs=[pl.BlockSpec((1, n2, 2 * F_WIDTH), lambda b, k: (b, 0, k)),
                  pl.BlockSpec((n2, 2 * n2), lambda b, k: (0, 0))],
        out_specs=pl.BlockSpec((1, n2, F_WIDTH), lambda b, k: (b, 0, k)),
        compiler_params=_cparams(("parallel", "parallel")),
        name="fft_c",
    )(x3, tc)
    return y.reshape(bsz * seq, F_WIDTH)


def _dft_tables(seq):
    n2 = FFT_N2
    n1 = seq // n2
    kc = np.arange(F_GROUP_DIM)
    ang = 2.0 * np.pi * ((kc[:, None] * kc[None, :]) % F_GROUP_DIM) / F_GROUP_DIM
    sc = 1.0 / math.sqrt(F_GROUP_DIM)
    fc = np.concatenate([np.cos(ang) * sc, -np.sin(ang) * sc], axis=1)
    k1 = np.arange(n1)[None, :, None]
    s1 = np.arange(n1)[None, None, :]
    s2 = np.arange(n2)[:, None, None]
    ang = 2.0 * np.pi * ((k1 * (s2 + n2 * s1)) % seq) / seq
    sa = 1.0 / math.sqrt(seq)
    ta = np.concatenate([np.cos(ang) * sa, np.sin(ang) * sa], axis=2)
    k2 = np.arange(n2)
    ang = 2.0 * np.pi * ((k2[:, None] * k2[None, :]) % n2) / n2
    tc = np.concatenate([np.cos(ang), np.sin(ang)], axis=1)
    as_bf16 = lambda a: jnp.asarray(a.astype(np.float32)).astype(BF16)
    return as_bf16(fc), as_bf16(ta), as_bf16(tc)


def _gla_levels(c):
    halves = []
    s = c // 2
    while s >= 1:
        halves.append(s)
        s //= 2
    return halves


def _gla_tables(c):
    r = np.arange(c)[:, None]
    l = np.arange(c)[None, :]
    mats = [(l <= r), (l > r)]
    masks = []
    for s in _gla_levels(c):
        m = (r // (2 * s)) * (2 * s) + s - 1
        q_side = r > m
        mats.append(np.where(q_side, (l > m) & (l <= r), (l > r) & (l <= m)))
        i, j = r, l
        masks.append((i // (2 * s) == j // (2 * s)) & (i % (2 * s) >= s) & (j % (2 * s) < s))
    masks.append(r == l)
    mt = np.stack(mats).astype(np.float32)
    mk = np.stack(masks).astype(np.float32)
    mt = np.stack([mt, mt[:, ::-1, ::-1]])
    mk = np.stack([mk, mk[:, ::-1, ::-1]])
    nt = mt.shape[1]
    return (jnp.asarray(mt.reshape(2, nt * c, c)).astype(BF16), jnp.asarray(mk))


def _gla_kernel(q_ref, k_ref, v_ref, a1_ref, w2h_ref, w2l_ref, ba_ref, mt_ref, mk_ref, o_ref, st_ref,
                *, chunk, n_lvl):
    c = chunk

    @pl.when(pl.program_id(3) == 0)
    def _():
        st_ref[...] = jnp.zeros_like(st_ref)

    a1h, a1l = _split_bf16(a1_ref[...])
    w2h = w2h_ref[0, 0]
    z = _dot(a1h, w2h) + _dot(a1l, w2h) + _dot(a1h, w2l_ref[0, 0]) + ba_ref[0, 0]
    la = jax.nn.log_sigmoid(z) * (1.0 / GATE_TAU)
    lah, lal = _split_bf16(la)
    ex = _dot(mt_ref[0], jnp.concatenate([lah, lal], axis=1))
    ex = jnp.exp(ex[:, :GLA_DK] + ex[:, GLA_DK:])

    qb = q_ref[...]
    kb = k_ref[...]
    v = v_ref[...]
    q = qb.astype(F32)
    k = kb.astype(F32)
    attn = lax.dot_general(qb, kb, _NT, preferred_element_type=F32) * mk_ref[0, n_lvl]
    for lvl in range(n_lvl):
        e = ex[(2 + lvl) * c:(3 + lvl) * c]
        a = lax.dot_general((q * e).astype(BF16), (k * e).astype(BF16), _NT, preferred_element_type=F32)
        attn = attn + a * mk_ref[0, lvl]
    e_cum = ex[0:c]
    e_rev = ex[c:2 * c]
    st = st_ref[...]
    o = _dot(attn.astype(BF16), v)
    o = o + lax.dot_general((q * e_cum).astype(BF16), st.astype(BF16), _NT, preferred_element_type=F32)
    o_ref[0] = o
    tot = e_cum[0:1] * e_rev[0:1]
    st_ref[...] = st * tot + lax.dot_general(v, (k * e_rev).astype(BF16), _TN, preferred_element_type=F32)


def _gla(p, a1, w2h, w2l, ba, mt, mk, bsz, seq):
    t = p.shape[0]
    c = min(GLA_CHUNK, seq)
    nc = seq // c
    n_lvl = len(_gla_levels(c))
    row = lambda d, b, c_: b * nc + c_ + d * (nc - 1 - 2 * c_)
    qoff = F_WIDTH // GLA_DK
    koff = (F_WIDTH + GLA_QK) // GLA_DK
    voff = (F_WIDTH + 2 * GLA_QK) // GLA_DV
    return pl.pallas_call(
        functools.partial(_gla_kernel, chunk=c, n_lvl=n_lvl),
        out_shape=jax.ShapeDtypeStruct((N_DIRS, t, GLA_V), F32),
        grid=(N_DIRS, bsz, GLA_HEADS, nc),
        in_specs=[pl.BlockSpec((c, GLA_DK), lambda d, b, h, c_: (row(d, b, c_), qoff + h)),
                  pl.BlockSpec((c, GLA_DK), lambda d, b, h, c_: (row(d, b, c_), koff + h)),
                  pl.BlockSpec((c, GLA_DV), lambda d, b, h, c_: (row(d, b, c_), voff + h)),
                  pl.BlockSpec((c, LANES), lambda d, b, h, c_: (row(d, b, c_), 0)),
                  pl.BlockSpec((1, 1, LANES, GLA_DK), lambda d, b, h, c_: (d, h, 0, 0)),
                  pl.BlockSpec((1, 1, LANES, GLA_DK), lambda d, b, h, c_: (d, h, 0, 0)),
                  pl.BlockSpec((1, 1, 1, GLA_DK), lambda d, b, h, c_: (d, h, 0, 0)),
                  pl.BlockSpec((1,) + mt.shape[1:], lambda d, b, h, c_: (d, 0, 0)),
                  pl.BlockSpec((1,) + mk.shape[1:], lambda d, b, h, c_: (d, 0, 0, 0))],
        out_specs=pl.BlockSpec((1, c, GLA_DV), lambda d, b, h, c_: (d, row(d, b, c_), h)),
        scratch_shapes=[pltpu.VMEM((GLA_DV, GLA_DK), F32)],
        compiler_params=_cparams(("parallel", "parallel", "parallel", "arbitrary")),
        name="gla",
    )(p, p, p, a1, w2h, w2l, ba, mt, mk)


def _mix_kernel(yf_ref, o_ref, r_ref, m_ref, x_ref, gn_ref, wf_ref, wg_ref, wo_ref,
                g1_ref, n2g_ref, sc2_ref, sh2_ref, h_ref, n2_ref):
    o = o_ref[0] + o_ref[1]
    parts = []
    for h in range(GLA_HEADS):
        oh = o[:, h * GLA_DV:(h + 1) * GLA_DV]
        ms = jnp.mean(oh * oh, axis=-1, keepdims=True)
        parts.append(oh * lax.rsqrt(ms + EPS * GLA_DK) * gn_ref[...])
    og = jnp.concatenate(parts, axis=1) * jax.nn.silu(r_ref[...].astype(F32))
    yg = _dot(og.astype(BF16), wg_ref[...])
    yf = _dot(yf_ref[...], wf_ref[...])
    d = yf.shape[1]
    m = m_ref[...].astype(F32)
    mixed = jax.nn.sigmoid(m[:, :d]) * yf + jax.nn.sigmoid(m[:, d:]) * yg
    h1 = x_ref[...] + g1_ref[0] * _dot(mixed.astype(BF16), wo_ref[...])
    h_ref[...] = h1
    ms = jnp.mean(h1 * h1, axis=-1, keepdims=True)
    n2 = h1 * lax.rsqrt(ms + EPS) * n2g_ref[...]
    n2_ref[...] = (n2 * (1.0 + sc2_ref[0]) + sh2_ref[0]).astype(BF16)


def _mix(yf, o, p, x2, gn, wf, wg, wo, mod3, n2g, seq, tm=256):
    t, d = x2.shape
    tm = min(tm, seq)
    bidx = lambda i: (i * tm) // seq
    const = lambda shape: pl.BlockSpec(shape, lambda i: (0,) * len(shape))
    roff = (F_WIDTH + 2 * GLA_QK + GLA_V) // GLA_V
    moff = (F_WIDTH + 2 * GLA_QK + 2 * GLA_V) // (2 * d)
    return pl.pallas_call(
        _mix_kernel,
        out_shape=(jax.ShapeDtypeStruct((t, d), F32), jax.ShapeDtypeStruct((t, d), BF16)),
        grid=(t // tm,),
        in_specs=[pl.BlockSpec((tm, F_WIDTH), lambda i: (i, 0)),
                  pl.BlockSpec((N_DIRS, tm, GLA_V), lambda i: (0, i, 0)),
                  pl.BlockSpec((tm, GLA_V), lambda i: (i, roff)),
                  pl.BlockSpec((tm, 2 * d), lambda i: (i, moff)),
                  pl.BlockSpec((tm, d), lambda i: (i, 0)),
                  const((1, GLA_DV)),
                  const((F_WIDTH, d)), const((GLA_V, d)), const((d, d)),
                  pl.BlockSpec((1, 1, d), lambda i: (bidx(i) * 6 + 2, 0, 0)),
                  const((1, d)),
                  pl.BlockSpec((1, 1, d), lambda i: (bidx(i) * 6 + 4, 0, 0)),
                  pl.BlockSpec((1, 1, d), lambda i: (bidx(i) * 6 + 3, 0, 0))],
        out_specs=(pl.BlockSpec((tm, d), lambda i: (i, 0)), pl.BlockSpec((tm, d), lambda i: (i, 0))),
        compiler_params=_cparams(("parallel",)),
        name="mix",
    )(yf, o, p, p, x2, gn.reshape(1, GLA_DV), wf, wg, wo, mod3, n2g.reshape(1, d), mod3, mod3)


def _top_values(s, n):
    rows = []
    cur = s
    for r in range(n):
        mx = jnp.max(cur, axis=0, keepdims=True)
        rows.append(mx)
        if r + 1 < n:
            cur = jnp.where(cur == mx, -jnp.inf, cur)
    return jnp.concatenate(rows, axis=0)


def _pair_candidates(e1, e2):
    rows = [e1[0:1] * e2]
    for a in range(1, 8):
        rows.append(e1[a:a + 1] * e2[0:8])
    rows.append(e1[8:16] * e2[0:1])
    return jnp.concatenate(rows, axis=0)


def _peer_sel_kernel(n2_ref, wqt_ref, keys_ref, e1_ref, e2_ref, th_ref):
    qt = lax.dot_general(wqt_ref[...], n2_ref[...], _NT, preferred_element_type=F32)
    for h in range(PEER_HEADS):
        base = h * 2 * PEER_HALF
        s1 = _dot(keys_ref[h, 0], qt[base:base + PEER_HALF].astype(BF16))
        s2 = _dot(keys_ref[h, 1], qt[base + PEER_HALF:base + 2 * PEER_HALF].astype(BF16))
        v1 = _top_values(s1, PEER_TOPK)
        v2 = _top_values(s2, PEER_TOPK)
        e1 = jnp.exp(v1 - v1[0:1])
        e2 = jnp.exp(v2 - v2[0:1])
        cand = _pair_candidates(e1, e2)
        cur = cand
        rem = jnp.full_like(cand[0:1], float(PEER_TOPK))
        theta = jnp.zeros_like(rem)
        for _ in range(PEER_TOPK):
            mx = jnp.max(cur, axis=0, keepdims=True)
            eq = cur == mx
            theta = jnp.where(rem > 0.0, mx, theta)
            rem = rem - jnp.sum(jnp.where(eq, 1.0, 0.0), axis=0, keepdims=True)
            cur = jnp.where(eq, -1.0, cur)
        sel = cand >= theta
        rz = 1.0 / jnp.sum(jnp.where(sel, cand, 0.0), axis=0, keepdims=True)
        candn = _pair_candidates(e1 * rz, e2)
        th_ref[h:h + 1, :] = jnp.min(jnp.where(sel, candn, jnp.inf), axis=0, keepdims=True)
        e1_ref[h] = jnp.exp(s1 - v1[0:1]) * rz
        e2_ref[h] = jnp.exp(s2 - v2[0:1])


def _peer_sel(n2, wqt, keys, tm=256):
    t, d = n2.shape
    tm = min(tm, t)
    nq = wqt.shape[0]
    return pl.pallas_call(
        _peer_sel_kernel,
        out_shape=(jax.ShapeDtypeStruct((PEER_HEADS, N_KEYS, t), F32),
                   jax.ShapeDtypeStruct((PEER_HEADS, N_KEYS, t), F32),
                   jax.ShapeDtypeStruct((PEER_HEADS, t), F32)),
        grid=(t // tm,),
        in_specs=[pl.BlockSpec((tm, d), lambda i: (i, 0)),
                  pl.BlockSpec((nq, d), lambda i: (0, 0)),
                  pl.BlockSpec((PEER_HEADS, 2, N_KEYS, PEER_HALF), lambda i: (0, 0, 0, 0))],
        out_specs=(pl.BlockSpec((PEER_HEADS, N_KEYS, tm), lambda i: (0, 0, i)),
                   pl.BlockSpec((PEER_HEADS, N_KEYS, tm), lambda i: (0, 0, i)),
                   pl.BlockSpec((PEER_HEADS, tm), lambda i: (0, i))),
        compiler_params=_cparams(("parallel",)),
        name="peer_sel",
    )(n2, wqt, keys)


def _peer_kernel(n2_ref, e1_ref, e2_ref, th_ref, u_ref, vt_ref, h1_ref, g2_ref, fg_ref, out_ref,
                 acc_ref, a_scr, w_scr, e1_scr, *, ib, tt):
    e = pl.program_id(1)

    @pl.when(e == 0)
    def _():
        acc_ref[...] = jnp.zeros_like(acc_ref)

    a_scr[...] = lax.dot_general(u_ref[...], n2_ref[...], _NT, preferred_element_type=F32)
    for h in range(PEER_HEADS):
        for ii in range(ib):
            e1_scr[ii, h, 0:1, :] = e1_ref[h, ii:ii + 1, :]

    def slab(ii, carry):
        r0 = pl.multiple_of(ii * N_KEYS, N_KEYS)
        for lc in range(tt // LANES):
            sl = slice(lc * LANES, (lc + 1) * LANES)
            g = jnp.zeros((N_KEYS, LANES), F32)
            for h in range(PEER_HEADS):
                p = e1_scr[ii, h, 0:1, sl] * e2_ref[h, :, sl]
                g = g + jnp.where(p >= th_ref[h:h + 1, sl], p, 0.0)
            a = a_scr[pl.ds(r0, N_KEYS), sl]
            act = 0.5 * a * (1.0 + lax.erf(a * (1.0 / math.sqrt(2.0))))
            w_scr[pl.ds(r0, N_KEYS), sl] = (act * g).astype(BF16)
        return carry

    lax.fori_loop(0, ib, slab, 0)
    acc_ref[...] += _dot(vt_ref[...], w_scr[...])

    @pl.when(e == pl.num_programs(1) - 1)
    def _():
        h2 = h1_ref[...] + g2_ref[0] * acc_ref[...].T
        ms = jnp.mean(h2 * h2, axis=-1, keepdims=True)
        out_ref[...] = h2 * lax.rsqrt(ms + EPS) * fg_ref[...]


def _peer(n2, e1, e2, th, u, vt, h1, mod3, fg, seq, tt=512, eb=1024):
    t, d = n2.shape
    tt = min(tt, seq)
    n_exp = u.shape[0]
    ib = eb // N_KEYS
    bidx = lambda i: (i * tt) // seq
    return pl.pallas_call(
        functools.partial(_peer_kernel, ib=ib, tt=tt),
        out_shape=jax.ShapeDtypeStruct((t, d), F32),
        grid=(t // tt, n_exp // eb),
        in_specs=[pl.BlockSpec((tt, d), lambda i, e: (i, 0)),
                  pl.BlockSpec((PEER_HEADS, ib, tt), lambda i, e: (0, e, i)),
                  pl.BlockSpec((PEER_HEADS, N_KEYS, tt), lambda i, e: (0, 0, i)),
                  pl.BlockSpec((PEER_HEADS, tt), lambda i, e: (0, i)),
                  pl.BlockSpec((eb, d), lambda i, e: (e, 0)),
                  pl.BlockSpec((d, eb), lambda i, e: (0, e)),
                  pl.BlockSpec((tt, d), lambda i, e: (i, 0), pipeline_mode=pl.Buffered(1)),
                  pl.BlockSpec((1, 1, d), lambda i, e: (bidx(i) * 6 + 5, 0, 0)),
                  pl.BlockSpec((1, d), lambda i, e: (0, 0))],
        out_specs=pl.BlockSpec((tt, d), lambda i, e: (i, 0)),
        scratch_shapes=[pltpu.VMEM((d, tt), F32), pltpu.VMEM((eb, tt), F32), pltpu.VMEM((eb, tt), BF16),
                        pltpu.VMEM((ib, PEER_HEADS, 8, tt), F32)],
        compiler_params=_cparams(("parallel", "arbitrary")),
        name="peer",
    )(n2, e1, e2, th, u, vt, h1, mod3, fg.reshape(1, d))


def kernel(x, c, w_ada, b_ada, norm1_g, w_in, w_fnet, gla_w_a2, gla_b_a, gla_norm_g, w_gla, w_out,
           norm2_g, peer_w_q, peer_keys, peer_u, peer_v, final_norm_g):
    bsz, seq, d = x.shape
    t = bsz * seq
    depth = w_ada.shape[0]
    fc, ta, tc = _dft_tables(seq)
    mt, mk = _gla_tables(min(GLA_CHUNK, seq))
    n_main = F_WIDTH + 2 * GLA_QK + 2 * GLA_V
    h = x.reshape(t, d)
    for l in range(depth):
        mod3 = _ada(c, w_ada[l], b_ada[l]).reshape(bsz * 6, 1, d)
        w_main = jnp.concatenate([w_in[l][:, :n_main], w_in[l][:, n_main + N_DIRS * GATE_RANK:]], axis=1)
        w_a1 = jnp.zeros((d, LANES), F32).at[:, :N_DIRS * GATE_RANK].set(
            w_in[l][:, n_main:n_main + N_DIRS * GATE_RANK])
        p, a1 = _inproj(h, norm1_g[l], mod3, w_main.astype(BF16), w_a1.astype(BF16), seq)

        yf = _seq_dft(_chan_dft(p, fc), bsz, seq, ta, tc)

        w2 = jnp.zeros((N_DIRS, LANES, GLA_QK), F32)
        for dr in range(N_DIRS):
            w2 = w2.at[dr, dr * GATE_RANK:(dr + 1) * GATE_RANK].set(gla_w_a2[l][dr])
        w2 = w2.reshape(N_DIRS, LANES, GLA_HEADS, GLA_DK).transpose(0, 2, 1, 3)
        w2h = w2.astype(BF16)
        w2l = (w2 - w2h.astype(F32)).astype(BF16)
        ba = gla_b_a[l].reshape(N_DIRS, GLA_HEADS, 1, GLA_DK)
        o = _gla(p, a1, w2h, w2l, ba, mt, mk, bsz, seq)

        h1, n2 = _mix(yf, o, p, h, gla_norm_g[l], w_fnet[l].astype(BF16), w_gla[l].astype(BF16),
                      w_out[l].astype(BF16), mod3, norm2_g[l], seq)

        e1, e2, th = _peer_sel(n2, peer_w_q[l].T.astype(BF16), peer_keys[l].astype(BF16))
        fg = final_norm_g if l == depth - 1 else jnp.ones((d,), F32)
        h = _peer(n2, e1, e2, th, peer_u[l].astype(BF16), peer_v[l].T.astype(BF16), h1, mod3, fg, seq)
        if l != depth - 1:
            raise NotImplementedError("only the final layer fuses its norm; depth > 1 is not supported")
    return h.reshape(bsz, seq, d)
```

```python
import functools
import math

import numpy as np
import jax
import jax.numpy as jnp
from jax import lax
from jax.experimental import pallas as pl
from jax.experimental.pallas import tpu as pltpu

BF16 = jnp.bfloat16
F32 = jnp.float32

EPS = 1e-6
F_GROUPS = 4
F_GROUP_DIM = 256
F_WIDTH = 1024
GLA_HEADS = 4
GLA_DK = 128
GLA_DV = 256
GLA_QK = 512
GLA_V = 1024
GATE_RANK = 16
GATE_TAU = 16.0
N_DIRS = 2
PEER_HEADS = 8
PEER_HALF = 128
N_KEYS = 128
PEER_TOPK = 16

LANES = 128
MXU_TILE = 256
UNIT_ROWS = 256
GLA_CHUNK = 128
FFT_N2 = 128
FFT_GROUP = 8
VMEM_LIMIT = 56 * 1024 * 1024

_NT = (((1,), (1,)), ((), ()))
_TN = (((0,), (0,)), ((), ()))


def _cparams(sem, flags=None):
    return pltpu.CompilerParams(dimension_semantics=sem, vmem_limit_bytes=VMEM_LIMIT, flags=flags)


def _dot(a, b):
    return jnp.dot(a, b, preferred_element_type=F32)


def _interleave(*streams):
    keyed = [((i + 0.5) / len(s), si, i, item) for si, s in enumerate(streams) for i, item in enumerate(s)]
    return [item for _, _, _, item in sorted(keyed, key=lambda t: t[:3])]


def _split_bf16(a):
    hi = a.astype(BF16)
    lo = (a - hi.astype(F32)).astype(BF16)
    return hi, lo


def _ada_kernel(c_ref, w_ref, b_ref, o_ref):
    s = jax.nn.silu(c_ref[...])
    o_ref[...] = _dot(s.astype(BF16), w_ref[...].astype(BF16)) + b_ref[...]


def _ada(c, w, b, tn=1536):
    bsz, d = c.shape
    n = w.shape[1]
    cp = jnp.zeros((8, d), F32).at[:bsz].set(c)
    out = pl.pallas_call(
        _ada_kernel,
        out_shape=jax.ShapeDtypeStruct((8, n), F32),
        grid=(n // tn,),
        in_specs=[pl.BlockSpec((8, d), lambda j: (0, 0)),
                  pl.BlockSpec((d, tn), lambda j: (0, j)),
                  pl.BlockSpec((1, tn), lambda j: (0, j))],
        out_specs=pl.BlockSpec((8, tn), lambda j: (0, j)),
        compiler_params=_cparams(("arbitrary",)),
        name="ada",
    )(cp, w, b.reshape(1, n))
    return out[:bsz]


def _inproj_kernel(x_ref, g_ref, sc_ref, sh_ref, w_ref, wa_ref, p_ref, a1_ref, n_scr):
    @pl.when(pl.program_id(1) == 0)
    def _():
        x = x_ref[...]
        ms = jnp.mean(x * x, axis=-1, keepdims=True)
        y = x * lax.rsqrt(ms + EPS) * g_ref[...]
        nb = (y * (1.0 + sc_ref[0]) + sh_ref[0]).astype(BF16)
        n_scr[...] = nb
        a1_ref[...] = _dot(nb, wa_ref[...])

    p_ref[...] = _dot(n_scr[...], w_ref[...]).astype(BF16)


def _inproj(x2, g, mod3, w, wa, seq, tm=1024, tn=1024):
    t, d = x2.shape
    n = w.shape[1]
    tm = min(tm, seq)
    bidx = lambda i: (i * tm) // seq
    return pl.pallas_call(
        _inproj_kernel,
        out_shape=(jax.ShapeDtypeStruct((t, n), BF16), jax.ShapeDtypeStruct((t, LANES), F32)),
        grid=(t // tm, n // tn),
        in_specs=[pl.BlockSpec((tm, d), lambda i, j: (i, 0)),
                  pl.BlockSpec((1, d), lambda i, j: (0, 0)),
                  pl.BlockSpec((1, 1, d), lambda i, j: (bidx(i) * 6 + 1, 0, 0)),
                  pl.BlockSpec((1, 1, d), lambda i, j: (bidx(i) * 6, 0, 0)),
                  pl.BlockSpec((d, tn), lambda i, j: (0, j)),
                  pl.BlockSpec((d, LANES), lambda i, j: (0, 0))],
        out_specs=(pl.BlockSpec((tm, tn), lambda i, j: (i, j)),
                   pl.BlockSpec((tm, LANES), lambda i, j: (i, 0))),
        scratch_shapes=[pltpu.VMEM((tm, d), BF16)],
        compiler_params=_cparams(("parallel", "arbitrary")),
        name="inproj",
    )(x2, g.reshape(1, d), mod3, mod3, w, wa)


def _chan_dft_kernel(z_ref, f_ref, w_ref):
    for g in range(F_GROUPS):
        lo, hi = g * F_GROUP_DIM, (g + 1) * F_GROUP_DIM
        r = _dot(z_ref[:, lo:hi], f_ref[...])
        w_ref[:, lo:hi] = r[:, :F_GROUP_DIM]
        w_ref[:, F_WIDTH + lo:F_WIDTH + hi] = r[:, F_GROUP_DIM:]


def _chan_dft(p, fc, tm=1024):
    t = p.shape[0]
    tm = min(tm, t)
    return pl.pallas_call(
        _chan_dft_kernel,
        out_shape=jax.ShapeDtypeStruct((t, 2 * F_WIDTH), F32),
        grid=(t // tm,),
        in_specs=[pl.BlockSpec((tm, F_WIDTH), lambda i: (i, 0)),
                  pl.BlockSpec((F_GROUP_DIM, 2 * F_GROUP_DIM), lambda i: (0, 0))],
        out_specs=pl.BlockSpec((tm, 2 * F_WIDTH), lambda i: (i, 0)),
        compiler_params=_cparams(("parallel",)),
        name="chan_dft",
    )(p, fc)


def _fft_a_kernel(w_ref, t_ref, x_ref):
    for j in range(FFT_GROUP):
        w = w_ref[0, :, j, :]
        rot = jnp.concatenate([w[:, F_WIDTH:], -w[:, :F_WIDTH]], axis=1)
        rhs = jnp.concatenate([w, rot], axis=0).astype(BF16)
        x_ref[0, j] = _dot(t_ref[j], rhs)


def _fft_c_kernel(x_ref, t_ref, y_ref):
    for j in range(FFT_GROUP):
        x = x_ref[0, :, j, :]
        rhs = jnp.concatenate([x[:, :F_WIDTH], x[:, F_WIDTH:]], axis=0).astype(BF16)
        y_ref[0, :, j, :] = _dot(t_ref[...], rhs)


def _seq_dft(wc, bsz, seq, ta, tc):
    n2 = FFT_N2
    n1 = seq // n2
    g = FFT_GROUP
    x1 = pl.pallas_call(
        _fft_a_kernel,
        out_shape=jax.ShapeDtypeStruct((bsz, n2, n1, 2 * F_WIDTH), F32),
        grid=(bsz, n2 // g),
        in_specs=[pl.BlockSpec((1, n1, g, 2 * F_WIDTH), lambda b, s: (b, 0, s, 0)),
                  pl.BlockSpec((g, n1, 2 * n1), lambda b, s: (s, 0, 0))],
        out_specs=pl.BlockSpec((1, g, n1, 2 * F_WIDTH), lambda b, s: (b, s, 0, 0)),
        compiler_params=_cparams(("parallel", "parallel")),
        name="fft_a",
    )(wc.reshape(bsz, n1, n2, 2 * F_WIDTH), ta)
    y = pl.pallas_call(
        _fft_c_kernel,
        out_shape=jax.ShapeDtypeStruct((bsz, n2, n1, F_WIDTH), F32),
        grid=(bsz, n1 // g),
        in_specs=[pl.BlockSpec((1, n2, g, 2 * F_WIDTH), lambda b, k: (b, 0, k, 0)),
                  pl.BlockSpec((n2, 2 * n2), lambda b, k: (0, 0))],
        out_specs=pl.BlockSpec((1, n2, g, F_WIDTH), lambda b, k: (b, 0, k, 0)),
        compiler_params=_cparams(("parallel", "parallel")),
        name="fft_c",
    )(x1, tc)
    return y.reshape(bsz * seq, F_WIDTH)


def _dft_tables(seq):
    n2 = FFT_N2
    n1 = seq // n2
    kc = np.arange(F_GROUP_DIM)
    ang = 2.0 * np.pi * ((kc[:, None] * kc[None, :]) % F_GROUP_DIM) / F_GROUP_DIM
    sc = 1.0 / math.sqrt(F_GROUP_DIM)
    fc = np.concatenate([np.cos(ang) * sc, -np.sin(ang) * sc], axis=1)
    k1 = np.arange(n1)[None, :, None]
    s1 = np.arange(n1)[None, None, :]
    s2 = np.arange(n2)[:, None, None]
    ang = 2.0 * np.pi * ((k1 * (s2 + n2 * s1)) % seq) / seq
    sa = 1.0 / math.sqrt(seq)
    ta = np.concatenate([np.cos(ang) * sa, np.sin(ang) * sa], axis=2)
    k2 = np.arange(n2)
    ang = 2.0 * np.pi * ((k2[:, None] * k2[None, :]) % n2) / n2
    tc = np.concatenate([np.cos(ang), np.sin(ang)], axis=1)
    as_bf16 = lambda a: jnp.asarray(a.astype(np.float32)).astype(BF16)
    return as_bf16(fc), as_bf16(ta), as_bf16(tc)


def _gla_levels(c):
    halves = []
    s = c // 2
    while s >= 1:
        halves.append(s)
        s //= 2
    return halves


def _gla_tables(c):
    r = np.arange(c)[:, None]
    l = np.arange(c)[None, :]
    mats = [(l <= r), (l > r)]
    masks = []
    for s in _gla_levels(c):
        m = (r // (2 * s)) * (2 * s) + s - 1
        q_side = r > m
        mats.append(np.where(q_side, (l > m) & (l <= r), (l > r) & (l <= m)))
        i, j = r, l
        masks.append((i // (2 * s) == j // (2 * s)) & (i % (2 * s) >= s) & (j % (2 * s) < s))
    masks.append(r == l)
    mt = np.stack(mats).astype(np.float32)
    mk = np.stack(masks).astype(np.float32)
    mt = np.stack([mt, mt[:, ::-1, ::-1]])
    mk = np.stack([mk, mk[:, ::-1, ::-1]])
    nt = mt.shape[1]
    return (jnp.asarray(mt.reshape(2, nt * c, c)).astype(BF16), jnp.asarray(mk))


def _gla_kernel(q_ref, k_ref, v_ref, a1_ref, w2h_ref, w2l_ref, ba_ref, mt_ref, mk_ref, o_ref, st_ref,
                *, chunk, n_lvl):
    c = chunk

    @pl.when(pl.program_id(3) == 0)
    def _():
        st_ref[...] = jnp.zeros_like(st_ref)

    a1h, a1l = _split_bf16(a1_ref[...])
    w2h = w2h_ref[0, 0]
    z = _dot(a1h, w2h) + _dot(a1l, w2h) + _dot(a1h, w2l_ref[0, 0]) + ba_ref[0, 0]
    la = jax.nn.log_sigmoid(z) * (1.0 / GATE_TAU)
    lah, lal = _split_bf16(la)
    ex = _dot(mt_ref[0], jnp.concatenate([lah, lal], axis=1))
    ex = jnp.exp(ex[:, :GLA_DK] + ex[:, GLA_DK:])

    qb = q_ref[...]
    kb = k_ref[...]
    v = v_ref[...]
    q = qb.astype(F32)
    k = kb.astype(F32)
    attn = lax.dot_general(qb, kb, _NT, preferred_element_type=F32) * mk_ref[0, n_lvl]
    for lvl in range(n_lvl):
        e = ex[(2 + lvl) * c:(3 + lvl) * c]
        a = lax.dot_general((q * e).astype(BF16), (k * e).astype(BF16), _NT, preferred_element_type=F32)
        attn = attn + a * mk_ref[0, lvl]
    e_cum = ex[0:c]
    e_rev = ex[c:2 * c]
    st = st_ref[...]
    o = _dot(attn.astype(BF16), v)
    o = o + lax.dot_general((q * e_cum).astype(BF16), st.astype(BF16), _NT, preferred_element_type=F32)
    o_ref[0] = o
    tot = e_cum[0:1] * e_rev[0:1]
    st_ref[...] = st * tot + lax.dot_general(v, (k * e_rev).astype(BF16), _TN, preferred_element_type=F32)


def _gla(p, a1, w2h, w2l, ba, mt, mk, bsz, seq):
    t = p.shape[0]
    c = min(GLA_CHUNK, seq)
    nc = seq // c
    n_lvl = len(_gla_levels(c))
    row = lambda d, b, c_: b * nc + c_ + d * (nc - 1 - 2 * c_)
    qoff = F_WIDTH // GLA_DK
    koff = (F_WIDTH + GLA_QK) // GLA_DK
    voff = (F_WIDTH + 2 * GLA_QK) // GLA_DV
    return pl.pallas_call(
        functools.partial(_gla_kernel, chunk=c, n_lvl=n_lvl),
        out_shape=jax.ShapeDtypeStruct((N_DIRS, t, GLA_V), F32),
        grid=(N_DIRS, bsz, GLA_HEADS, nc),
        in_specs=[pl.BlockSpec((c, GLA_DK), lambda d, b, h, c_: (row(d, b, c_), qoff + h)),
                  pl.BlockSpec((c, GLA_DK), lambda d, b, h, c_: (row(d, b, c_), koff + h)),
                  pl.BlockSpec((c, GLA_DV), lambda d, b, h, c_: (row(d, b, c_), voff + h)),
                  pl.BlockSpec((c, LANES), lambda d, b, h, c_: (row(d, b, c_), 0)),
                  pl.BlockSpec((1, 1, LANES, GLA_DK), lambda d, b, h, c_: (d, h, 0, 0)),
                  pl.BlockSpec((1, 1, LANES, GLA_DK), lambda d, b, h, c_: (d, h, 0, 0)),
                  pl.BlockSpec((1, 1, 1, GLA_DK), lambda d, b, h, c_: (d, h, 0, 0)),
                  pl.BlockSpec((1,) + mt.shape[1:], lambda d, b, h, c_: (d, 0, 0)),
                  pl.BlockSpec((1,) + mk.shape[1:], lambda d, b, h, c_: (d, 0, 0, 0))],
        out_specs=pl.BlockSpec((1, c, GLA_DV), lambda d, b, h, c_: (d, row(d, b, c_), h)),
        scratch_shapes=[pltpu.VMEM((GLA_DV, GLA_DK), F32)],
        compiler_params=_cparams(("parallel", "parallel", "parallel", "arbitrary")),
        name="gla",
    )(p, p, p, a1, w2h, w2l, ba, mt, mk)


def _mix_kernel(yf_ref, o_ref, r_ref, m_ref, x_ref, gn_ref, wf_ref, wg_ref, wo_ref,
                g1_ref, n2g_ref, sc2_ref, sh2_ref, h_ref, n2t_ref):
    o = o_ref[0] + o_ref[1]
    parts = []
    for h in range(GLA_HEADS):
        oh = o[:, h * GLA_DV:(h + 1) * GLA_DV]
        ms = jnp.mean(oh * oh, axis=-1, keepdims=True)
        parts.append(oh * lax.rsqrt(ms + EPS * GLA_DK) * gn_ref[...])
    og = jnp.concatenate(parts, axis=1) * jax.nn.silu(r_ref[...].astype(F32))
    yg = _dot(og.astype(BF16), wg_ref[...])
    yf = _dot(yf_ref[...].astype(BF16), wf_ref[...])
    d = yf.shape[1]
    m = m_ref[...].astype(F32)
    mixed = jax.nn.sigmoid(m[:, :d]) * yf + jax.nn.sigmoid(m[:, d:]) * yg
    h1 = x_ref[...] + g1_ref[0] * _dot(mixed.astype(BF16), wo_ref[...])
    h_ref[...] = h1
    ms = jnp.mean(h1 * h1, axis=-1, keepdims=True)
    n2 = h1 * lax.rsqrt(ms + EPS) * n2g_ref[...]
    n2t_ref[...] = (n2 * (1.0 + sc2_ref[0]) + sh2_ref[0]).T.astype(BF16)


def _mix(yf, o, p, x2, gn, wf, wg, wo, mod3, n2g, seq, tm=256):
    t, d = x2.shape
    tm = min(tm, seq)
    bidx = lambda i: (i * tm) // seq
    const = lambda shape: pl.BlockSpec(shape, lambda i: (0,) * len(shape))
    roff = (F_WIDTH + 2 * GLA_QK + GLA_V) // GLA_V
    moff = (F_WIDTH + 2 * GLA_QK + 2 * GLA_V) // (2 * d)
    return pl.pallas_call(
        _mix_kernel,
        out_shape=(jax.ShapeDtypeStruct((t, d), F32), jax.ShapeDtypeStruct((d, t), BF16)),
        grid=(t // tm,),
        in_specs=[pl.BlockSpec((tm, F_WIDTH), lambda i: (i, 0)),
                  pl.BlockSpec((N_DIRS, tm, GLA_V), lambda i: (0, i, 0)),
                  pl.BlockSpec((tm, GLA_V), lambda i: (i, roff)),
                  pl.BlockSpec((tm, 2 * d), lambda i: (i, moff)),
                  pl.BlockSpec((tm, d), lambda i: (i, 0)),
                  const((1, GLA_DV)),
                  const((F_WIDTH, d)), const((GLA_V, d)), const((d, d)),
                  pl.BlockSpec((1, 1, d), lambda i: (bidx(i) * 6 + 2, 0, 0)),
                  const((1, d)),
                  pl.BlockSpec((1, 1, d), lambda i: (bidx(i) * 6 + 4, 0, 0)),
                  pl.BlockSpec((1, 1, d), lambda i: (bidx(i) * 6 + 3, 0, 0))],
        out_specs=(pl.BlockSpec((tm, d), lambda i: (i, 0)), pl.BlockSpec((d, tm), lambda i: (0, i))),
        compiler_params=_cparams(("parallel",)),
        name="mix",
    )(yf, o, p, p, x2, gn.reshape(1, GLA_DV), wf, wg, wo, mod3, n2g.reshape(1, d), mod3, mod3)


def _top_values(s, n):
    rows = []
    cur = s
    for r in range(n):
        mx = jnp.max(cur, axis=0, keepdims=True)
        rows.append(mx)
        if r + 1 < n:
            cur = jnp.where(cur == mx, -jnp.inf, cur)
    return jnp.concatenate(rows, axis=0)


def _pair_candidates(e1, e2):
    rows = [e1[0:1] * e2]
    for a in range(1, 8):
        rows.append(e1[a:a + 1] * e2[0:8])
    rows.append(e1[8:16] * e2[0:1])
    return jnp.concatenate(rows, axis=0)


def _peer_sel_kernel(n2t_ref, wqt_ref, keys_ref, e1_ref, e2_ref, th_ref):
    qt = _dot(wqt_ref[...], n2t_ref[...])
    for h in range(PEER_HEADS):
        base = h * 2 * PEER_HALF
        s1 = _dot(keys_ref[h, 0], qt[base:base + PEER_HALF].astype(BF16))
        s2 = _dot(keys_ref[h, 1], qt[base + PEER_HALF:base + 2 * PEER_HALF].astype(BF16))
        v1 = _top_values(s1, PEER_TOPK)
        v2 = _top_values(s2, PEER_TOPK)
        e1 = jnp.exp(v1 - v1[0:1])
        e2 = jnp.exp(v2 - v2[0:1])
        cand = _pair_candidates(e1, e2)
        cur = cand
        rem = jnp.full_like(cand[0:1], float(PEER_TOPK))
        theta = jnp.zeros_like(rem)
        for _ in range(PEER_TOPK):
            mx = jnp.max(cur, axis=0, keepdims=True)
            eq = cur == mx
            theta = jnp.where(rem > 0.0, mx, theta)
            rem = rem - jnp.sum(jnp.where(eq, 1.0, 0.0), axis=0, keepdims=True)
            cur = jnp.where(eq, -1.0, cur)
        sel = cand >= theta
        rz = 1.0 / jnp.sum(jnp.where(sel, cand, 0.0), axis=0, keepdims=True)
        candn = _pair_candidates(e1 * rz, e2)
        th_ref[h:h + 1, :] = jnp.min(jnp.where(sel, candn, jnp.inf), axis=0, keepdims=True)
        e1_ref[h] = jnp.exp(s1 - v1[0:1]) * rz
        e2_ref[h] = jnp.exp(s2 - v2[0:1])


def _peer_sel(n2t, wqt, keys, tm=256):
    d, t = n2t.shape
    tm = min(tm, t)
    nq = wqt.shape[0]
    return pl.pallas_call(
        _peer_sel_kernel,
        out_shape=(jax.ShapeDtypeStruct((PEER_HEADS, N_KEYS, t), F32),
                   jax.ShapeDtypeStruct((PEER_HEADS, N_KEYS, t), F32),
                   jax.ShapeDtypeStruct((PEER_HEADS, t), F32)),
        grid=(t // tm,),
        in_specs=[pl.BlockSpec((d, tm), lambda i: (0, i)),
                  pl.BlockSpec((nq, d), lambda i: (0, 0)),
                  pl.BlockSpec((PEER_HEADS, 2, N_KEYS, PEER_HALF), lambda i: (0, 0, 0, 0))],
        out_specs=(pl.BlockSpec((PEER_HEADS, N_KEYS, tm), lambda i: (0, 0, i)),
                   pl.BlockSpec((PEER_HEADS, N_KEYS, tm), lambda i: (0, 0, i)),
                   pl.BlockSpec((PEER_HEADS, tm), lambda i: (0, i))),
        compiler_params=_cparams(("parallel",)),
        name="peer_sel",
    )(n2t, wqt, keys)


def _peer_kernel(n2t_ref, e1_ref, e2_ref, th_ref, u_ref, vt_ref, h1_ref, g2_ref, fg_ref, out_ref,
                 acc_ref, a0_scr, a1_scr, w0_scr, w1_scr, *, n_e, ib, tt):
    k = pl.program_id(0)
    e_out = lax.rem(jnp.maximum(k - 2, 0), n_e)

    @pl.when(k == 0)
    def _():
        a1_scr[...] = jnp.zeros_like(a1_scr)
        w0_scr[...] = jnp.zeros_like(w0_scr)

    @pl.when((k == 0) | ((k >= 2) & (e_out == 0)))
    def _():
        acc_ref[...] = jnp.zeros_like(acc_ref)

    def gate_unit(a_prv, w_prv, ii, lc):
        rows = slice(ii * N_KEYS, (ii + 1) * N_KEYS)
        sl = slice(lc * LANES, (lc + 1) * LANES)
        g = jnp.zeros((N_KEYS, LANES), F32)
        for h in range(PEER_HEADS):
            p = e1_ref[h, ii:ii + 1, sl] * e2_ref[h, :, sl]
            g = g + jnp.where(p >= th_ref[h:h + 1, sl], p, 0.0)
        a = a_prv[rows, sl]
        act = 0.5 * a * (1.0 + lax.erf(a * (1.0 / math.sqrt(2.0))))
        w_prv[rows, sl] = (act * g).astype(BF16)

    def stages(a_cur, a_prv, w_cur, w_prv):
        eb, d = a_cur.shape[0], acc_ref.shape[0]

        def pre_unit(r, c):
            rows, toks = slice(r * UNIT_ROWS, (r + 1) * UNIT_ROWS), slice(c * MXU_TILE, (c + 1) * MXU_TILE)
            a_cur[rows, toks] = _dot(u_ref[rows, :], n2t_ref[:, toks])

        def val_unit(r, c):
            rows, toks = slice(r * UNIT_ROWS, (r + 1) * UNIT_ROWS), slice(c * MXU_TILE, (c + 1) * MXU_TILE)
            acc_ref[rows, toks] += _dot(vt_ref[rows, :], w_cur[:, toks])

        n_tc = tt // MXU_TILE
        pre = [functools.partial(pre_unit, r, c) for r in range(eb // UNIT_ROWS) for c in range(n_tc)]
        val = [functools.partial(val_unit, r, c) for r in range(d // UNIT_ROWS) for c in range(n_tc)]
        gate = [functools.partial(gate_unit, a_prv, w_prv, ii, lc) for ii in range(ib) for lc in range(tt // LANES)]
        for unit in _interleave(pre, val, gate):
            unit()

    @pl.when(lax.rem(k, 2) == 0)
    def _():
        stages(a0_scr, a1_scr, w0_scr, w1_scr)

    @pl.when(lax.rem(k, 2) == 1)
    def _():
        stages(a1_scr, a0_scr, w1_scr, w0_scr)

    @pl.when((k >= 2) & (e_out == n_e - 1))
    def _():
        h2 = h1_ref[...] + g2_ref[0] * acc_ref[...].T
        ms = jnp.mean(h2 * h2, axis=-1, keepdims=True)
        out_ref[...] = h2 * lax.rsqrt(ms + EPS) * fg_ref[...]


def _peer(n2t, e1, e2, th, u, vt, h1, mod3, fg, seq, tt=512, eb=1024):
    d, t = n2t.shape
    tt = min(tt, seq)
    n_e = u.shape[0] // eb
    n_blk = (t // tt) * n_e
    ib = eb // N_KEYS
    s1 = lambda k: jnp.minimum(k, n_blk - 1)
    s2 = lambda k: jnp.clip(k - 1, 0, n_blk - 1)
    s3 = lambda k: jnp.clip(k - 2, 0, n_blk - 1)
    bidx = lambda i: (i * tt) // seq
    return pl.pallas_call(
        functools.partial(_peer_kernel, n_e=n_e, ib=ib, tt=tt),
        out_shape=jax.ShapeDtypeStruct((t, d), F32),
        grid=(n_blk + 2,),
        in_specs=[pl.BlockSpec((d, tt), lambda k: (0, s1(k) // n_e), pipeline_mode=pl.Buffered(1)),
                  pl.BlockSpec((PEER_HEADS, ib, tt), lambda k: (0, s2(k) % n_e, s2(k) // n_e)),
                  pl.BlockSpec((PEER_HEADS, N_KEYS, tt), lambda k: (0, 0, s2(k) // n_e)),
                  pl.BlockSpec((PEER_HEADS, tt), lambda k: (0, s2(k) // n_e)),
                  pl.BlockSpec((eb, d), lambda k: (s1(k) % n_e, 0)),
                  pl.BlockSpec((d, eb), lambda k: (0, s3(k) % n_e)),
                  pl.BlockSpec((tt, d), lambda k: (s3(k) // n_e, 0), pipeline_mode=pl.Buffered(1)),
                  pl.BlockSpec((1, 1, d), lambda k: (bidx(s3(k) // n_e) * 6 + 5, 0, 0)),
                  pl.BlockSpec((1, d), lambda k: (0, 0))],
        out_specs=pl.BlockSpec((tt, d), lambda k: (s3(k) // n_e, 0)),
        scratch_shapes=[pltpu.VMEM((d, tt), F32), pltpu.VMEM((eb, tt), F32), pltpu.VMEM((eb, tt), F32),
                        pltpu.VMEM((eb, tt), BF16), pltpu.VMEM((eb, tt), BF16)],
        compiler_params=_cparams(("arbitrary",)),
        name="peer",
    )(n2t, e1, e2, th, u, vt, h1, mod3, fg.reshape(1, d))


def kernel(x, c, w_ada, b_ada, norm1_g, w_in, w_fnet, gla_w_a2, gla_b_a, gla_norm_g, w_gla, w_out,
           norm2_g, peer_w_q, peer_keys, peer_u, peer_v, final_norm_g):
    bsz, seq, d = x.shape
    t = bsz * seq
    depth = w_ada.shape[0]
    fc, ta, tc = _dft_tables(seq)
    mt, mk = _gla_tables(min(GLA_CHUNK, seq))
    n_main = F_WIDTH + 2 * GLA_QK + 2 * GLA_V
    h = x.reshape(t, d)
    for l in range(depth):
        mod3 = _ada(c, w_ada[l], b_ada[l]).reshape(bsz * 6, 1, d)
        w_main = jnp.concatenate([w_in[l][:, :n_main], w_in[l][:, n_main + N_DIRS * GATE_RANK:]], axis=1)
        w_a1 = jnp.zeros((d, LANES), F32).at[:, :N_DIRS * GATE_RANK].set(
            w_in[l][:, n_main:n_main + N_DIRS * GATE_RANK])
        p, a1 = _inproj(h, norm1_g[l], mod3, w_main.astype(BF16), w_a1.astype(BF16), seq)

        yf = _seq_dft(_chan_dft(p, fc), bsz, seq, ta, tc)

        w2 = jnp.zeros((N_DIRS, LANES, GLA_QK), F32)
        for dr in range(N_DIRS):
            w2 = w2.at[dr, dr * GATE_RANK:(dr + 1) * GATE_RANK].set(gla_w_a2[l][dr])
        w2 = w2.reshape(N_DIRS, LANES, GLA_HEADS, GLA_DK).transpose(0, 2, 1, 3)
        w2h = w2.astype(BF16)
        w2l = (w2 - w2h.astype(F32)).astype(BF16)
        ba = gla_b_a[l].reshape(N_DIRS, GLA_HEADS, 1, GLA_DK)
        o = _gla(p, a1, w2h, w2l, ba, mt, mk, bsz, seq)

        h1, n2t = _mix(yf, o, p, h, gla_norm_g[l], w_fnet[l].astype(BF16), w_gla[l].astype(BF16),
                      w_out[l].astype(BF16), mod3, norm2_g[l], seq)

        e1, e2, th = _peer_sel(n2t, peer_w_q[l].T.astype(BF16), peer_keys[l].astype(BF16))
        fg = final_norm_g if l == depth - 1 else jnp.ones((d,), F32)
        h = _peer(n2t, e1, e2, th, peer_u[l].astype(BF16), peer_v[l].T.astype(BF16), h1, mod3, fg, seq)
        if l != depth - 1:
            raise NotImplementedError("only the final layer fuses its norm; depth > 1 is not supported")
    return h.reshape(bsz, seq, d)
```

```python
import functools
import math

import numpy as np
import jax
import jax.numpy as jnp
from jax import lax
from jax.experimental import pallas as pl
from jax.experimental.pallas import tpu as pltpu

BF16 = jnp.bfloat16
F32 = jnp.float32

EPS = 1e-6
F_GROUPS = 4
F_GROUP_DIM = 256
F_WIDTH = 1024
GLA_HEADS = 4
GLA_DK = 128
GLA_DV = 256
GLA_QK = 512
GLA_V = 1024
GATE_RANK = 16
GATE_TAU = 16.0
N_DIRS = 2
PEER_HEADS = 8
PEER_HALF = 128
N_KEYS = 128
PEER_TOPK = 16

LANES = 128
MXU_TILE = 256
UNIT_ROWS = 256
GLA_CHUNK = 128
FFT_N2 = 128
FFT_GROUP = 8
VMEM_LIMIT = 56 * 1024 * 1024

_NT = (((1,), (1,)), ((), ()))
_TN = (((0,), (0,)), ((), ()))


def _cparams(sem, flags=None):
    return pltpu.CompilerParams(dimension_semantics=sem, vmem_limit_bytes=VMEM_LIMIT, flags=flags)


def _dot(a, b):
    return jnp.dot(a, b, preferred_element_type=F32)


def _interleave(*streams):
    keyed = [((i + 0.5) / len(s), si, i, item) for si, s in enumerate(streams) for i, item in enumerate(s)]
    return [item for _, _, _, item in sorted(keyed, key=lambda t: t[:3])]


def _split_bf16(a):
    hi = a.astype(BF16)
    lo = (a - hi.astype(F32)).astype(BF16)
    return hi, lo


def _ada_kernel(c_ref, w_ref, b_ref, o_ref):
    s = jax.nn.silu(c_ref[...])
    o_ref[...] = _dot(s.astype(BF16), w_ref[...].astype(BF16)) + b_ref[...]


def _ada(c, w, b, tn=1536):
    bsz, d = c.shape
    n = w.shape[1]
    cp = jnp.zeros((8, d), F32).at[:bsz].set(c)
    out = pl.pallas_call(
        _ada_kernel,
        out_shape=jax.ShapeDtypeStruct((8, n), F32),
        grid=(n // tn,),
        in_specs=[pl.BlockSpec((8, d), lambda j: (0, 0)),
                  pl.BlockSpec((d, tn), lambda j: (0, j)),
                  pl.BlockSpec((1, tn), lambda j: (0, j))],
        out_specs=pl.BlockSpec((8, tn), lambda j: (0, j)),
        compiler_params=_cparams(("arbitrary",)),
        name="ada",
    )(cp, w, b.reshape(1, n))
    return out[:bsz]


def _inproj_kernel(x_ref, g_ref, sc_ref, sh_ref, w_ref, wa_ref, p_ref, a1_ref, n_scr):
    @pl.when(pl.program_id(1) == 0)
    def _():
        x = x_ref[...]
        ms = jnp.mean(x * x, axis=-1, keepdims=True)
        y = x * lax.rsqrt(ms + EPS) * g_ref[...]
        nb = (y * (1.0 + sc_ref[0]) + sh_ref[0]).astype(BF16)
        n_scr[...] = nb
        a1_ref[...] = _dot(nb, wa_ref[...])

    p_ref[...] = _dot(n_scr[...], w_ref[...]).astype(BF16)


def _inproj(x2, g, mod3, w, wa, seq, tm=1024, tn=1024):
    t, d = x2.shape
    n = w.shape[1]
    tm = min(tm, seq)
    bidx = lambda i: (i * tm) // seq
    return pl.pallas_call(
        _inproj_kernel,
        out_shape=(jax.ShapeDtypeStruct((t, n), BF16), jax.ShapeDtypeStruct((t, LANES), F32)),
        grid=(t // tm, n // tn),
        in_specs=[pl.BlockSpec((tm, d), lambda i, j: (i, 0)),
                  pl.BlockSpec((1, d), lambda i, j: (0, 0)),
                  pl.BlockSpec((1, 1, d), lambda i, j: (bidx(i) * 6 + 1, 0, 0)),
                  pl.BlockSpec((1, 1, d), lambda i, j: (bidx(i) * 6, 0, 0)),
                  pl.BlockSpec((d, tn), lambda i, j: (0, j)),
                  pl.BlockSpec((d, LANES), lambda i, j: (0, 0))],
        out_specs=(pl.BlockSpec((tm, tn), lambda i, j: (i, j)),
                   pl.BlockSpec((tm, LANES), lambda i, j: (i, 0))),
        scratch_shapes=[pltpu.VMEM((tm, d), BF16)],
        compiler_params=_cparams(("parallel", "arbitrary")),
        name="inproj",
    )(x2, g.reshape(1, d), mod3, mod3, w, wa)


def _chan_dft_kernel(z_ref, f_ref, w_ref):
    for g in range(F_GROUPS):
        lo, hi = g * F_GROUP_DIM, (g + 1) * F_GROUP_DIM
        r = _dot(z_ref[:, lo:hi], f_ref[...])
        w_ref[:, lo:hi] = r[:, :F_GROUP_DIM]
        w_ref[:, F_WIDTH + lo:F_WIDTH + hi] = r[:, F_GROUP_DIM:]


def _chan_dft(p, fc, tm=1024):
    t = p.shape[0]
    tm = min(tm, t)
    return pl.pallas_call(
        _chan_dft_kernel,
        out_shape=jax.ShapeDtypeStruct((t, 2 * F_WIDTH), F32),
        grid=(t // tm,),
        in_specs=[pl.BlockSpec((tm, F_WIDTH), lambda i: (i, 0)),
                  pl.BlockSpec((F_GROUP_DIM, 2 * F_GROUP_DIM), lambda i: (0, 0))],
        out_specs=pl.BlockSpec((tm, 2 * F_WIDTH), lambda i: (i, 0)),
        compiler_params=_cparams(("parallel",)),
        name="chan_dft",
    )(p, fc)


def _fft_a_kernel(w_ref, t_ref, x_ref):
    for j in range(FFT_GROUP):
        w = w_ref[0, :, j, :]
        rot = jnp.concatenate([w[:, F_WIDTH:], -w[:, :F_WIDTH]], axis=1)
        rhs = jnp.concatenate([w, rot], axis=0).astype(BF16)
        x_ref[0, j] = _dot(t_ref[j], rhs)


def _fft_c_kernel(x_ref, t_ref, y_ref):
    for j in range(FFT_GROUP):
        x = x_ref[0, :, j, :]
        rhs = jnp.concatenate([x[:, :F_WIDTH], x[:, F_WIDTH:]], axis=0).astype(BF16)
        y_ref[0, :, j, :] = _dot(t_ref[...], rhs)


def _seq_dft(wc, bsz, seq, ta, tc):
    n2 = FFT_N2
    n1 = seq // n2
    g = FFT_GROUP
    x1 = pl.pallas_call(
        _fft_a_kernel,
        out_shape=jax.ShapeDtypeStruct((bsz, n2, n1, 2 * F_WIDTH), F32),
        grid=(bsz, n2 // g),
        in_specs=[pl.BlockSpec((1, n1, g, 2 * F_WIDTH), lambda b, s: (b, 0, s, 0)),
                  pl.BlockSpec((g, n1, 2 * n1), lambda b, s: (s, 0, 0))],
        out_specs=pl.BlockSpec((1, g, n1, 2 * F_WIDTH), lambda b, s: (b, s, 0, 0)),
        compiler_params=_cparams(("parallel", "parallel")),
        name="fft_a",
    )(wc.reshape(bsz, n1, n2, 2 * F_WIDTH), ta)
    y = pl.pallas_call(
        _fft_c_kernel,
        out_shape=jax.ShapeDtypeStruct((bsz, n2, n1, F_WIDTH), F32),
        grid=(bsz, n1 // g),
        in_specs=[pl.BlockSpec((1, n2, g, 2 * F_WIDTH), lambda b, k: (b, 0, k, 0)),
                  pl.BlockSpec((n2, 2 * n2), lambda b, k: (0, 0))],
        out_specs=pl.BlockSpec((1, n2, g, F_WIDTH), lambda b, k: (b, 0, k, 0)),
        compiler_params=_cparams(("parallel", "parallel")),
        name="fft_c",
    )(x1, tc)
    return y.reshape(bsz * seq, F_WIDTH)


def _dft_tables(seq):
    n2 = FFT_N2
    n1 = seq // n2
    kc = np.arange(F_GROUP_DIM)
    ang = 2.0 * np.pi * ((kc[:, None] * kc[None, :]) % F_GROUP_DIM) / F_GROUP_DIM
    sc = 1.0 / math.sqrt(F_GROUP_DIM)
    fc = np.concatenate([np.cos(ang) * sc, -np.sin(ang) * sc], axis=1)
    k1 = np.arange(n1)[None, :, None]
    s1 = np.arange(n1)[None, None, :]
    s2 = np.arange(n2)[:, None, None]
    ang = 2.0 * np.pi * ((k1 * (s2 + n2 * s1)) % seq) / seq
    sa = 1.0 / math.sqrt(seq)
    ta = np.concatenate([np.cos(ang) * sa, np.sin(ang) * sa], axis=2)
    k2 = np.arange(n2)
    ang = 2.0 * np.pi * ((k2[:, None] * k2[None, :]) % n2) / n2
    tc = np.concatenate([np.cos(ang), np.sin(ang)], axis=1)
    as_bf16 = lambda a: jnp.asarray(a.astype(np.float32)).astype(BF16)
    return as_bf16(fc), as_bf16(ta), as_bf16(tc)


def _gla_levels(c):
    halves = []
    s = c // 2
    while s >= 1:
        halves.append(s)
        s //= 2
    return halves


def _gla_tables(c):
    r = np.arange(c)[:, None]
    l = np.arange(c)[None, :]
    mats = [(l <= r), (l > r)]
    masks = []
    for s in _gla_levels(c):
        m = (r // (2 * s)) * (2 * s) + s - 1
        q_side = r > m
        mats.append(np.where(q_side, (l > m) & (l <= r), (l > r) & (l <= m)))
        i, j = r, l
        masks.append((i // (2 * s) == j // (2 * s)) & (i % (2 * s) >= s) & (j % (2 * s) < s))
    masks.append(r == l)
    mt = np.stack(mats).astype(np.float32)
    mk = np.stack(masks).astype(np.float32)
    mt = np.stack([mt, mt[:, ::-1, ::-1]])
    mk = np.stack([mk, mk[:, ::-1, ::-1]])
    nt = mt.shape[1]
    return (jnp.asarray(mt.reshape(2, nt * c, c)).astype(BF16), jnp.asarray(mk))


def _gla_kernel(q_ref, k_ref, v_ref, a1_ref, w2h_ref, w2l_ref, ba_ref, mt_ref, mk_ref, o_ref, st_ref,
                *, chunk, n_lvl):
    c = chunk
    heads = range(GLA_HEADS)
    dk = lambda h: slice(h * GLA_DK, (h + 1) * GLA_DK)
    dv = lambda h: slice(h * GLA_DV, (h + 1) * GLA_DV)

    @pl.when(pl.program_id(2) == 0)
    def _():
        st_ref[...] = jnp.zeros_like(st_ref)

    a1h, a1l = _split_bf16(a1_ref[...])
    w2h = w2h_ref[0]
    z = _dot(a1h, w2h) + _dot(a1l, w2h) + _dot(a1h, w2l_ref[0]) + ba_ref[0]
    la = jax.nn.log_sigmoid(z) * (1.0 / GATE_TAU)
    lah, lal = _split_bf16(la)
    ex = _dot(mt_ref[0], jnp.concatenate([lah, lal], axis=1))
    ex = jnp.exp(ex[:, :GLA_QK] + ex[:, GLA_QK:])

    qb = q_ref[...]
    kb = k_ref[...]
    q = qb.astype(F32)
    k = kb.astype(F32)
    attn = [lax.dot_general(qb[:, dk(h)], kb[:, dk(h)], _NT, preferred_element_type=F32) * mk_ref[0, n_lvl]
            for h in heads]
    for lvl in range(n_lvl):
        e = ex[(2 + lvl) * c:(3 + lvl) * c]
        qe = (q * e).astype(BF16)
        ke = (k * e).astype(BF16)
        for h in heads:
            a = lax.dot_general(qe[:, dk(h)], ke[:, dk(h)], _NT, preferred_element_type=F32)
            attn[h] = attn[h] + a * mk_ref[0, lvl]
    e_cum = ex[0:c]
    e_rev = ex[c:2 * c]
    qc = (q * e_cum).astype(BF16)
    kd = (k * e_rev).astype(BF16)
    tot = e_cum[0:1] * e_rev[0:1]
    for h in heads:
        v = v_ref[:, dv(h)]
        st = st_ref[h]
        o = _dot(attn[h].astype(BF16), v)
        o = o + lax.dot_general(qc[:, dk(h)], st.astype(BF16), _NT, preferred_element_type=F32)
        o_ref[0, :, dv(h)] = o
        st_ref[h] = st * tot[:, dk(h)] + lax.dot_general(v, kd[:, dk(h)], _TN, preferred_element_type=F32)


def _gla(p, a1, w2h, w2l, ba, mt, mk, bsz, seq):
    t = p.shape[0]
    c = min(GLA_CHUNK, seq)
    nc = seq // c
    n_lvl = len(_gla_levels(c))
    row = lambda d, b, c_: b * nc + c_ + d * (nc - 1 - 2 * c_)
    qoff = F_WIDTH // GLA_QK
    koff = (F_WIDTH + GLA_QK) // GLA_QK
    voff = (F_WIDTH + 2 * GLA_QK) // GLA_V
    return pl.pallas_call(
        functools.partial(_gla_kernel, chunk=c, n_lvl=n_lvl),
        out_shape=jax.ShapeDtypeStruct((N_DIRS, t, GLA_V), F32),
        grid=(N_DIRS, bsz, nc),
        in_specs=[pl.BlockSpec((c, GLA_QK), lambda d, b, c_: (row(d, b, c_), qoff)),
                  pl.BlockSpec((c, GLA_QK), lambda d, b, c_: (row(d, b, c_), koff)),
                  pl.BlockSpec((c, GLA_V), lambda d, b, c_: (row(d, b, c_), voff)),
                  pl.BlockSpec((c, LANES), lambda d, b, c_: (row(d, b, c_), 0)),
                  pl.BlockSpec((1, LANES, GLA_QK), lambda d, b, c_: (d, 0, 0)),
                  pl.BlockSpec((1, LANES, GLA_QK), lambda d, b, c_: (d, 0, 0)),
                  pl.BlockSpec((1, 1, GLA_QK), lambda d, b, c_: (d, 0, 0)),
                  pl.BlockSpec((1,) + mt.shape[1:], lambda d, b, c_: (d, 0, 0)),
                  pl.BlockSpec((1,) + mk.shape[1:], lambda d, b, c_: (d, 0, 0, 0))],
        out_specs=pl.BlockSpec((1, c, GLA_V), lambda d, b, c_: (d, row(d, b, c_), 0)),
        scratch_shapes=[pltpu.VMEM((GLA_HEADS, GLA_DV, GLA_DK), F32)],
        compiler_params=_cparams(("parallel", "parallel", "arbitrary")),
        name="gla",
    )(p, p, p, a1, w2h, w2l, ba, mt, mk)


def _mix_kernel(yf_ref, o_ref, r_ref, m_ref, x_ref, gn_ref, wf_ref, wg_ref, wo_ref,
                g1_ref, n2g_ref, sc2_ref, sh2_ref, h_ref, n2t_ref):
    o = o_ref[0] + o_ref[1]
    parts = []
    for h in range(GLA_HEADS):
        oh = o[:, h * GLA_DV:(h + 1) * GLA_DV]
        ms = jnp.mean(oh * oh, axis=-1, keepdims=True)
        parts.append(oh * lax.rsqrt(ms + EPS * GLA_DK) * gn_ref[...])
    og = jnp.concatenate(parts, axis=1) * jax.nn.silu(r_ref[...].astype(F32))
    yg = _dot(og.astype(BF16), wg_ref[...])
    yf = _dot(yf_ref[...].astype(BF16), wf_ref[...])
    d = yf.shape[1]
    m = m_ref[...].astype(F32)
    mixed = jax.nn.sigmoid(m[:, :d]) * yf + jax.nn.sigmoid(m[:, d:]) * yg
    h1 = x_ref[...] + g1_ref[0] * _dot(mixed.astype(BF16), wo_ref[...])
    h_ref[...] = h1
    ms = jnp.mean(h1 * h1, axis=-1, keepdims=True)
    n2 = h1 * lax.rsqrt(ms + EPS) * n2g_ref[...]
    n2t_ref[...] = (n2 * (1.0 + sc2_ref[0]) + sh2_ref[0]).T.astype(BF16)


def _mix(yf, o, p, x2, gn, wf, wg, wo, mod3, n2g, seq, tm=256):
    t, d = x2.shape
    tm = min(tm, seq)
    bidx = lambda i: (i * tm) // seq
    const = lambda shape: pl.BlockSpec(shape, lambda i: (0,) * len(shape))
    roff = (F_WIDTH + 2 * GLA_QK + GLA_V) // GLA_V
    moff = (F_WIDTH + 2 * GLA_QK + 2 * GLA_V) // (2 * d)
    return pl.pallas_call(
        _mix_kernel,
        out_shape=(jax.ShapeDtypeStruct((t, d), F32), jax.ShapeDtypeStruct((d, t), BF16)),
        grid=(t // tm,),
        in_specs=[pl.BlockSpec((tm, F_WIDTH), lambda i: (i, 0)),
                  pl.BlockSpec((N_DIRS, tm, GLA_V), lambda i: (0, i, 0)),
                  pl.BlockSpec((tm, GLA_V), lambda i: (i, roff)),
                  pl.BlockSpec((tm, 2 * d), lambda i: (i, moff)),
                  pl.BlockSpec((tm, d), lambda i: (i, 0)),
                  const((1, GLA_DV)),
                  const((F_WIDTH, d)), const((GLA_V, d)), const((d, d)),
                  pl.BlockSpec((1, 1, d), lambda i: (bidx(i) * 6 + 2, 0, 0)),
                  const((1, d)),
                  pl.BlockSpec((1, 1, d), lambda i: (bidx(i) * 6 + 4, 0, 0)),
                  pl.BlockSpec((1, 1, d), lambda i: (bidx(i) * 6 + 3, 0, 0))],
        out_specs=(pl.BlockSpec((tm, d), lambda i: (i, 0)), pl.BlockSpec((d, tm), lambda i: (0, i))),
        compiler_params=_cparams(("parallel",)),
        name="mix",
    )(yf, o, p, p, x2, gn.reshape(1, GLA_DV), wf, wg, wo, mod3, n2g.reshape(1, d), mod3, mod3)


def _top_values(s, n):
    rows = []
    cur = s
    for r in range(n):
        mx = jnp.max(cur, axis=0, keepdims=True)
        rows.append(mx)
        if r + 1 < n:
            cur = jnp.where(cur == mx, -jnp.inf, cur)
    return jnp.concatenate(rows, axis=0)


def _pair_candidates(e1, e2):
    rows = [e1[0:1] * e2]
    for a in range(1, 8):
        rows.append(e1[a:a + 1] * e2[0:8])
    rows.append(e1[8:16] * e2[0:1])
    return jnp.concatenate(rows, axis=0)


def _peer_sel_kernel(n2t_ref, wqt_ref, keys_ref, e1_ref, e2_ref, th_ref):
    qt = _dot(wqt_ref[...], n2t_ref[...])
    for h in range(PEER_HEADS):
        base = h * 2 * PEER_HALF
        s1 = _dot(keys_ref[h, 0], qt[base:base + PEER_HALF].astype(BF16))
        s2 = _dot(keys_ref[h, 1], qt[base + PEER_HALF:base + 2 * PEER_HALF].astype(BF16))
        v1 = _top_values(s1, PEER_TOPK)
        v2 = _top_values(s2, PEER_TOPK)
        e1 = jnp.exp(v1 - v1[0:1])
        e2 = jnp.exp(v2 - v2[0:1])
        cand = _pair_candidates(e1, e2)
        cur = cand
        rem = jnp.full_like(cand[0:1], float(PEER_TOPK))
        theta = jnp.zeros_like(rem)
        for _ in range(PEER_TOPK):
            mx = jnp.max(cur, axis=0, keepdims=True)
            eq = cur == mx
            theta = jnp.where(rem > 0.0, mx, theta)
            rem = rem - jnp.sum(jnp.where(eq, 1.0, 0.0), axis=0, keepdims=True)
            cur = jnp.where(eq, -1.0, cur)
        sel = cand >= theta
        rz = 1.0 / jnp.sum(jnp.where(sel, cand, 0.0), axis=0, keepdims=True)
        candn = _pair_candidates(e1 * rz, e2)
        th_ref[h:h + 1, :] = jnp.min(jnp.where(sel, candn, jnp.inf), axis=0, keepdims=True)
        e1_ref[h] = jnp.exp(s1 - v1[0:1]) * rz
        e2_ref[h] = jnp.exp(s2 - v2[0:1])


def _peer_sel(n2t, wqt, keys, tm=256):
    d, t = n2t.shape
    tm = min(tm, t)
    nq = wqt.shape[0]
    return pl.pallas_call(
        _peer_sel_kernel,
        out_shape=(jax.ShapeDtypeStruct((PEER_HEADS, N_KEYS, t), F32),
                   jax.ShapeDtypeStruct((PEER_HEADS, N_KEYS, t), F32),
                   jax.ShapeDtypeStruct((PEER_HEADS, t), F32)),
        grid=(t // tm,),
        in_specs=[pl.BlockSpec((d, tm), lambda i: (0, i)),
                  pl.BlockSpec((nq, d), lambda i: (0, 0)),
                  pl.BlockSpec((PEER_HEADS, 2, N_KEYS, PEER_HALF), lambda i: (0, 0, 0, 0))],
        out_specs=(pl.BlockSpec((PEER_HEADS, N_KEYS, tm), lambda i: (0, 0, i)),
                   pl.BlockSpec((PEER_HEADS, N_KEYS, tm), lambda i: (0, 0, i)),
                   pl.BlockSpec((PEER_HEADS, tm), lambda i: (0, i))),
        compiler_params=_cparams(("parallel",)),
        name="peer_sel",
    )(n2t, wqt, keys)


def _peer_kernel(n2t_ref, e1_ref, e2_ref, th_ref, u_ref, vt_ref, h1_ref, g2_ref, fg_ref, out_ref,
                 acc_ref, a0_scr, a1_scr, w0_scr, w1_scr, *, n_e, ib, tt):
    k = pl.program_id(0)
    e_out = lax.rem(jnp.maximum(k - 2, 0), n_e)

    @pl.when(k == 0)
    def _():
        a1_scr[...] = jnp.zeros_like(a1_scr)
        w0_scr[...] = jnp.zeros_like(w0_scr)

    @pl.when((k == 0) | ((k >= 2) & (e_out == 0)))
    def _():
        acc_ref[...] = jnp.zeros_like(acc_ref)

    def gate_unit(a_prv, w_prv, ii, lc):
        rows = slice(ii * N_KEYS, (ii + 1) * N_KEYS)
        sl = slice(lc * LANES, (lc + 1) * LANES)
        g = jnp.zeros((N_KEYS, LANES), F32)
        for h in range(PEER_HEADS):
            p = e1_ref[h, ii:ii + 1, sl] * e2_ref[h, :, sl]
            g = g + jnp.where(p >= th_ref[h:h + 1, sl], p, 0.0)
        a = a_prv[rows, sl]
        act = 0.5 * a * (1.0 + lax.erf(a * (1.0 / math.sqrt(2.0))))
        w_prv[rows, sl] = (act * g).astype(BF16)

    def stages(a_cur, a_prv, w_cur, w_prv):
        eb, d = a_cur.shape[0], acc_ref.shape[0]

        def pre_unit(r, c):
            rows, toks = slice(r * UNIT_ROWS, (r + 1) * UNIT_ROWS), slice(c * MXU_TILE, (c + 1) * MXU_TILE)
            a_cur[rows, toks] = _dot(u_ref[rows, :], n2t_ref[:, toks])

        def val_unit(r, c):
            rows, toks = slice(r * UNIT_ROWS, (r + 1) * UNIT_ROWS), slice(c * MXU_TILE, (c + 1) * MXU_TILE)
            acc_ref[rows, toks] += _dot(vt_ref[0, rows, :], w_cur[:, toks])

        n_tc = tt // MXU_TILE
        pre = [functools.partial(pre_unit, r, c) for r in range(eb // UNIT_ROWS) for c in range(n_tc)]
        val = [functools.partial(val_unit, r, c) for r in range(d // UNIT_ROWS) for c in range(n_tc)]
        gate = [functools.partial(gate_unit, a_prv, w_prv, ii, lc) for ii in range(ib) for lc in range(tt // LANES)]
        for unit in _interleave(pre, val, gate):
            unit()

    @pl.when(lax.rem(k, 2) == 0)
    def _():
        stages(a0_scr, a1_scr, w0_scr, w1_scr)

    @pl.when(lax.rem(k, 2) == 1)
    def _():
        stages(a1_scr, a0_scr, w1_scr, w0_scr)

    @pl.when((k >= 2) & (e_out == n_e - 1))
    def _():
        h2 = h1_ref[...] + g2_ref[0] * acc_ref[...].T
        ms = jnp.mean(h2 * h2, axis=-1, keepdims=True)
        out_ref[...] = h2 * lax.rsqrt(ms + EPS) * fg_ref[...]


def _peer(n2t, e1, e2, th, u, v, h1, mod3, fg, seq, tt=512, eb=1024):
    d, t = n2t.shape
    tt = min(tt, seq)
    n_e = u.shape[0] // eb
    vt = v.reshape(n_e, eb, d).transpose(0, 2, 1)
    n_blk = (t // tt) * n_e
    ib = eb // N_KEYS
    s1 = lambda k: jnp.minimum(k, n_blk - 1)
    s2 = lambda k: jnp.clip(k - 1, 0, n_blk - 1)
    s3 = lambda k: jnp.clip(k - 2, 0, n_blk - 1)
    bidx = lambda i: (i * tt) // seq
    return pl.pallas_call(
        functools.partial(_peer_kernel, n_e=n_e, ib=ib, tt=tt),
        out_shape=jax.ShapeDtypeStruct((t, d), F32),
        grid=(n_blk + 2,),
        in_specs=[pl.BlockSpec((d, tt), lambda k: (0, s1(k) // n_e), pipeline_mode=pl.Buffered(1)),
                  pl.BlockSpec((PEER_HEADS, ib, tt), lambda k: (0, s2(k) % n_e, s2(k) // n_e)),
                  pl.BlockSpec((PEER_HEADS, N_KEYS, tt), lambda k: (0, 0, s2(k) // n_e)),
                  pl.BlockSpec((PEER_HEADS, tt), lambda k: (0, s2(k) // n_e)),
                  pl.BlockSpec((eb, d), lambda k: (s1(k) % n_e, 0)),
                  pl.BlockSpec((1, d, eb), lambda k: (s3(k) % n_e, 0, 0)),
                  pl.BlockSpec((tt, d), lambda k: (s3(k) // n_e, 0), pipeline_mode=pl.Buffered(1)),
                  pl.BlockSpec((1, 1, d), lambda k: (bidx(s3(k) // n_e) * 6 + 5, 0, 0)),
                  pl.BlockSpec((1, d), lambda k: (0, 0))],
        out_specs=pl.BlockSpec((tt, d), lambda k: (s3(k) // n_e, 0)),
        scratch_shapes=[pltpu.VMEM((d, tt), F32), pltpu.VMEM((eb, tt), F32), pltpu.VMEM((eb, tt), F32),
                        pltpu.VMEM((eb, tt), BF16), pltpu.VMEM((eb, tt), BF16)],
        compiler_params=_cparams(("arbitrary",)),
        name="peer",
    )(n2t, e1, e2, th, u, vt, h1, mod3, fg.reshape(1, d))


def kernel(x, c, w_ada, b_ada, norm1_g, w_in, w_fnet, gla_w_a2, gla_b_a, gla_norm_g, w_gla, w_out,
           norm2_g, peer_w_q, peer_keys, peer_u, peer_v, final_norm_g):
    bsz, seq, d = x.shape
    t = bsz * seq
    depth = w_ada.shape[0]
    fc, ta, tc = _dft_tables(seq)
    mt, mk = _gla_tables(min(GLA_CHUNK, seq))
    n_main = F_WIDTH + 2 * GLA_QK + 2 * GLA_V
    h = x.reshape(t, d)
    for l in range(depth):
        mod3 = _ada(c, w_ada[l], b_ada[l]).reshape(bsz * 6, 1, d)
        w_main = jnp.concatenate([w_in[l][:, :n_main], w_in[l][:, n_main + N_DIRS * GATE_RANK:]], axis=1)
        w_a1 = jnp.zeros((d, LANES), F32).at[:, :N_DIRS * GATE_RANK].set(
            w_in[l][:, n_main:n_main + N_DIRS * GATE_RANK])
        p, a1 = _inproj(h, norm1_g[l], mod3, w_main.astype(BF16), w_a1.astype(BF16), seq)

        yf = _seq_dft(_chan_dft(p, fc), bsz, seq, ta, tc)

        w2 = jnp.zeros((N_DIRS, LANES, GLA_QK), F32)
        for dr in range(N_DIRS):
            w2 = w2.at[dr, dr * GATE_RANK:(dr + 1) * GATE_RANK].set(gla_w_a2[l][dr])
        w2h = w2.astype(BF16)
        w2l = (w2 - w2h.astype(F32)).astype(BF16)
        ba = gla_b_a[l].reshape(N_DIRS, 1, GLA_QK)
        o = _gla(p, a1, w2h, w2l, ba, mt, mk, bsz, seq)

        h1, n2t = _mix(yf, o, p, h, gla_norm_g[l], w_fnet[l].astype(BF16), w_gla[l].astype(BF16),
                      w_out[l].astype(BF16), mod3, norm2_g[l], seq)

        e1, e2, th = _peer_sel(n2t, peer_w_q[l].T.astype(BF16), peer_keys[l].astype(BF16))
        fg = final_norm_g if l == depth - 1 else jnp.ones((d,), F32)
        h = _peer(n2t, e1, e2, th, peer_u[l].astype(BF16), peer_v[l].astype(BF16), h1, mod3, fg, seq)
        if l != depth - 1:
            raise NotImplementedError("only the final layer fuses its norm; depth > 1 is not supported")
    return h.reshape(bsz, seq, d)
```

```python
import functools
import math

import numpy as np
import jax
import jax.numpy as jnp
from jax import lax
from jax.experimental import pallas as pl
from jax.experimental.pallas import tpu as pltpu

BF16 = jnp.bfloat16
F32 = jnp.float32

EPS = 1e-6
F_GROUPS = 4
F_GROUP_DIM = 256
F_WIDTH = 1024
GLA_HEADS = 4
GLA_DK = 128
GLA_DV = 256
GLA_QK = 512
GLA_V = 1024
GATE_RANK = 16
GATE_TAU = 16.0
N_DIRS = 2
PEER_HEADS = 8
PEER_HALF = 128
N_KEYS = 128
PEER_TOPK = 16

LANES = 128
MXU_TILE = 256
UNIT_ROWS = 256
GLA_CHUNK = 128
FFT_N2 = 128
FFT_GROUP = 8
VMEM_LIMIT = 56 * 1024 * 1024

_NT = (((1,), (1,)), ((), ()))
_TN = (((0,), (0,)), ((), ()))


def _cparams(sem, flags=None):
    return pltpu.CompilerParams(dimension_semantics=sem, vmem_limit_bytes=VMEM_LIMIT, flags=flags)


def _dot(a, b):
    return jnp.dot(a, b, preferred_element_type=F32)


def _interleave(*streams):
    keyed = [((i + 0.5) / len(s), si, i, item) for si, s in enumerate(streams) for i, item in enumerate(s)]
    return [item for _, _, _, item in sorted(keyed, key=lambda t: t[:3])]


def _split_bf16(a):
    hi = a.astype(BF16)
    lo = (a - hi.astype(F32)).astype(BF16)
    return hi, lo


def _ada_kernel(c_ref, w_ref, b_ref, o_ref):
    s = jax.nn.silu(c_ref[...])
    o_ref[...] = _dot(s.astype(BF16), w_ref[...].astype(BF16)) + b_ref[...]


def _ada(c, w, b, tn=1536):
    bsz, d = c.shape
    n = w.shape[1]
    cp = jnp.zeros((8, d), F32).at[:bsz].set(c)
    out = pl.pallas_call(
        _ada_kernel,
        out_shape=jax.ShapeDtypeStruct((8, n), F32),
        grid=(n // tn,),
        in_specs=[pl.BlockSpec((8, d), lambda j: (0, 0)),
                  pl.BlockSpec((d, tn), lambda j: (0, j)),
                  pl.BlockSpec((1, tn), lambda j: (0, j))],
        out_specs=pl.BlockSpec((8, tn), lambda j: (0, j)),
        compiler_params=_cparams(("arbitrary",)),
        name="ada",
    )(cp, w, b.reshape(1, n))
    return out[:bsz]


def _inproj_kernel(x_ref, g_ref, sc_ref, sh_ref, w_ref, wa_ref, p_ref, a1_ref, n_scr):
    @pl.when(pl.program_id(1) == 0)
    def _():
        x = x_ref[...]
        ms = jnp.mean(x * x, axis=-1, keepdims=True)
        y = x * lax.rsqrt(ms + EPS) * g_ref[...]
        nb = (y * (1.0 + sc_ref[0]) + sh_ref[0]).astype(BF16)
        n_scr[...] = nb
        a1_ref[...] = _dot(nb, wa_ref[...])

    p_ref[...] = _dot(n_scr[...], w_ref[...]).astype(BF16)


def _inproj(x2, g, mod3, w, wa, seq, tm=1024, tn=1024):
    t, d = x2.shape
    n = w.shape[1]
    tm = min(tm, seq)
    bidx = lambda i: (i * tm) // seq
    return pl.pallas_call(
        _inproj_kernel,
        out_shape=(jax.ShapeDtypeStruct((t, n), BF16), jax.ShapeDtypeStruct((t, LANES), F32)),
        grid=(t // tm, n // tn),
        in_specs=[pl.BlockSpec((tm, d), lambda i, j: (i, 0)),
                  pl.BlockSpec((1, d), lambda i, j: (0, 0)),
                  pl.BlockSpec((1, 1, d), lambda i, j: (bidx(i) * 6 + 1, 0, 0)),
                  pl.BlockSpec((1, 1, d), lambda i, j: (bidx(i) * 6, 0, 0)),
                  pl.BlockSpec((d, tn), lambda i, j: (0, j)),
                  pl.BlockSpec((d, LANES), lambda i, j: (0, 0))],
        out_specs=(pl.BlockSpec((tm, tn), lambda i, j: (i, j)),
                   pl.BlockSpec((tm, LANES), lambda i, j: (i, 0))),
        scratch_shapes=[pltpu.VMEM((tm, d), BF16)],
        compiler_params=_cparams(("parallel", "arbitrary")),
        name="inproj",
    )(x2, g.reshape(1, d), mod3, mod3, w, wa)


def _chan_dft_kernel(z_ref, f_ref, w_ref):
    for g in range(F_GROUPS):
        lo, hi = g * F_GROUP_DIM, (g + 1) * F_GROUP_DIM
        r = _dot(z_ref[:, lo:hi], f_ref[...])
        w_ref[:, lo:hi] = r[:, :F_GROUP_DIM]
        w_ref[:, F_WIDTH + lo:F_WIDTH + hi] = r[:, F_GROUP_DIM:]


def _chan_dft(p, fc, tm=1024):
    t = p.shape[0]
    tm = min(tm, t)
    return pl.pallas_call(
        _chan_dft_kernel,
        out_shape=jax.ShapeDtypeStruct((t, 2 * F_WIDTH), F32),
        grid=(t // tm,),
        in_specs=[pl.BlockSpec((tm, F_WIDTH), lambda i: (i, 0)),
                  pl.BlockSpec((F_GROUP_DIM, 2 * F_GROUP_DIM), lambda i: (0, 0))],
        out_specs=pl.BlockSpec((tm, 2 * F_WIDTH), lambda i: (i, 0)),
        compiler_params=_cparams(("parallel",)),
        name="chan_dft",
    )(p, fc)


def _fft_a_kernel(w_ref, t_ref, x_ref):
    for j in range(FFT_GROUP):
        w = w_ref[0, :, j, :]
        rot = jnp.concatenate([w[:, F_WIDTH:], -w[:, :F_WIDTH]], axis=1)
        rhs = jnp.concatenate([w, rot], axis=0).astype(BF16)
        x_ref[0, j] = _dot(t_ref[j], rhs)


def _fft_c_kernel(x_ref, t_ref, y_ref):
    for j in range(FFT_GROUP):
        x = x_ref[0, :, j, :]
        rhs = jnp.concatenate([x[:, :F_WIDTH], x[:, F_WIDTH:]], axis=0).astype(BF16)
        y_ref[0, :, j, :] = _dot(t_ref[...], rhs)


def _seq_dft(wc, bsz, seq, ta, tc):
    n2 = FFT_N2
    n1 = seq // n2
    g = FFT_GROUP
    x1 = pl.pallas_call(
        _fft_a_kernel,
        out_shape=jax.ShapeDtypeStruct((bsz, n2, n1, 2 * F_WIDTH), F32),
        grid=(bsz, n2 // g),
        in_specs=[pl.BlockSpec((1, n1, g, 2 * F_WIDTH), lambda b, s: (b, 0, s, 0)),
                  pl.BlockSpec((g, n1, 2 * n1), lambda b, s: (s, 0, 0))],
        out_specs=pl.BlockSpec((1, g, n1, 2 * F_WIDTH), lambda b, s: (b, s, 0, 0)),
        compiler_params=_cparams(("parallel", "parallel")),
        name="fft_a",
    )(wc.reshape(bsz, n1, n2, 2 * F_WIDTH), ta)
    y = pl.pallas_call(
        _fft_c_kernel,
        out_shape=jax.ShapeDtypeStruct((bsz, n2, n1, F_WIDTH), F32),
        grid=(bsz, n1 // g),
        in_specs=[pl.BlockSpec((1, n2, g, 2 * F_WIDTH), lambda b, k: (b, 0, k, 0)),
                  pl.BlockSpec((n2, 2 * n2), lambda b, k: (0, 0))],
        out_specs=pl.BlockSpec((1, n2, g, F_WIDTH), lambda b, k: (b, 0, k, 0)),
        compiler_params=_cparams(("parallel", "parallel")),
        name="fft_c",
    )(x1, tc)
    return y.reshape(bsz * seq, F_WIDTH)


def _dft_tables(seq):
    n2 = FFT_N2
    n1 = seq // n2
    kc = np.arange(F_GROUP_DIM)
    ang = 2.0 * np.pi * ((kc[:, None] * kc[None, :]) % F_GROUP_DIM) / F_GROUP_DIM
    sc = 1.0 / math.sqrt(F_GROUP_DIM)
    fc = np.concatenate([np.cos(ang) * sc, -np.sin(ang) * sc], axis=1)
    k1 = np.arange(n1)[None, :, None]
    s1 = np.arange(n1)[None, None, :]
    s2 = np.arange(n2)[:, None, None]
    ang = 2.0 * np.pi * ((k1 * (s2 + n2 * s1)) % seq) / seq
    sa = 1.0 / math.sqrt(seq)
    ta = np.concatenate([np.cos(ang) * sa, np.sin(ang) * sa], axis=2)
    k2 = np.arange(n2)
    ang = 2.0 * np.pi * ((k2[:, None] * k2[None, :]) % n2) / n2
    tc = np.concatenate([np.cos(ang), np.sin(ang)], axis=1)
    as_bf16 = lambda a: jnp.asarray(a.astype(np.float32)).astype(BF16)
    return as_bf16(fc), as_bf16(ta), as_bf16(tc)


def _gla_levels(c):
    halves = []
    s = c // 2
    while s >= 1:
        halves.append(s)
        s //= 2
    return halves


def _gla_tables(c):
    r = np.arange(c)[:, None]
    l = np.arange(c)[None, :]
    mats = [(l <= r), (l > r)]
    masks = []
    for s in _gla_levels(c):
        m = (r // (2 * s)) * (2 * s) + s - 1
        q_side = r > m
        mats.append(np.where(q_side, (l > m) & (l <= r), (l > r) & (l <= m)))
        i, j = r, l
        masks.append((i // (2 * s) == j // (2 * s)) & (i % (2 * s) >= s) & (j % (2 * s) < s))
    masks.append(r == l)
    mt = np.stack(mats).astype(np.float32)
    mk = np.stack(masks).astype(np.float32)
    mt = np.stack([mt, mt[:, ::-1, ::-1]])
    mk = np.stack([mk, mk[:, ::-1, ::-1]])
    nt = mt.shape[1]
    return (jnp.asarray(mt.reshape(2, nt * c, c)).astype(BF16), jnp.asarray(mk))


def _gla_kernel(q_ref, k_ref, v_ref, a1_ref, w2h_ref, w2l_ref, ba_ref, mt_ref, mk_ref, o_ref, st_ref,
                *, chunk, n_lvl):
    c = chunk
    heads = range(GLA_HEADS)
    dk = lambda h: slice(h * GLA_DK, (h + 1) * GLA_DK)
    dv = lambda h: slice(h * GLA_DV, (h + 1) * GLA_DV)

    @pl.when(pl.program_id(2) == 0)
    def _():
        st_ref[...] = jnp.zeros_like(st_ref)

    a1h, a1l = _split_bf16(a1_ref[...])
    w2h = w2h_ref[0]
    z = _dot(a1h, w2h) + _dot(a1l, w2h) + _dot(a1h, w2l_ref[0]) + ba_ref[0]
    la = jax.nn.log_sigmoid(z) * (1.0 / GATE_TAU)
    lah, lal = _split_bf16(la)
    ex = _dot(mt_ref[0], jnp.concatenate([lah, lal], axis=1))
    ex = jnp.exp(ex[:, :GLA_QK] + ex[:, GLA_QK:])

    qb = q_ref[...]
    kb = k_ref[...]
    q = qb.astype(F32)
    k = kb.astype(F32)
    attn = [lax.dot_general(qb[:, dk(h)], kb[:, dk(h)], _NT, preferred_element_type=F32) * mk_ref[0, n_lvl]
            for h in heads]
    for lvl in range(n_lvl):
        e = ex[(2 + lvl) * c:(3 + lvl) * c]
        qe = (q * e).astype(BF16)
        ke = (k * e).astype(BF16)
        for h in heads:
            a = lax.dot_general(qe[:, dk(h)], ke[:, dk(h)], _NT, preferred_element_type=F32)
            attn[h] = attn[h] + a * mk_ref[0, lvl]
    e_cum = ex[0:c]
    e_rev = ex[c:2 * c]
    qc = (q * e_cum).astype(BF16)
    kd = (k * e_rev).astype(BF16)
    tot = e_cum[0:1] * e_rev[0:1]
    for h in heads:
        v = v_ref[:, dv(h)]
        st = st_ref[h]
        o = _dot(attn[h].astype(BF16), v)
        o = o + lax.dot_general(qc[:, dk(h)], st.astype(BF16), _NT, preferred_element_type=F32)
        o_ref[0, :, dv(h)] = o
        st_ref[h] = st * tot[:, dk(h)] + lax.dot_general(v, kd[:, dk(h)], _TN, preferred_element_type=F32)


def _gla(p, a1, w2h, w2l, ba, mt, mk, bsz, seq):
    t = p.shape[0]
    c = min(GLA_CHUNK, seq)
    nc = seq // c
    n_lvl = len(_gla_levels(c))
    row = lambda d, b, c_: b * nc + c_ + d * (nc - 1 - 2 * c_)
    qoff = F_WIDTH // GLA_QK
    koff = (F_WIDTH + GLA_QK) // GLA_QK
    voff = (F_WIDTH + 2 * GLA_QK) // GLA_V
    return pl.pallas_call(
        functools.partial(_gla_kernel, chunk=c, n_lvl=n_lvl),
        out_shape=jax.ShapeDtypeStruct((N_DIRS, t, GLA_V), F32),
        grid=(N_DIRS, bsz, nc),
        in_specs=[pl.BlockSpec((c, GLA_QK), lambda d, b, c_: (row(d, b, c_), qoff)),
                  pl.BlockSpec((c, GLA_QK), lambda d, b, c_: (row(d, b, c_), koff)),
                  pl.BlockSpec((c, GLA_V), lambda d, b, c_: (row(d, b, c_), voff)),
                  pl.BlockSpec((c, LANES), lambda d, b, c_: (row(d, b, c_), 0)),
                  pl.BlockSpec((1, LANES, GLA_QK), lambda d, b, c_: (d, 0, 0)),
                  pl.BlockSpec((1, LANES, GLA_QK), lambda d, b, c_: (d, 0, 0)),
                  pl.BlockSpec((1, 1, GLA_QK), lambda d, b, c_: (d, 0, 0)),
                  pl.BlockSpec((1,) + mt.shape[1:], lambda d, b, c_: (d, 0, 0)),
                  pl.BlockSpec((1,) + mk.shape[1:], lambda d, b, c_: (d, 0, 0, 0))],
        out_specs=pl.BlockSpec((1, c, GLA_V), lambda d, b, c_: (d, row(d, b, c_), 0)),
        scratch_shapes=[pltpu.VMEM((GLA_HEADS, GLA_DV, GLA_DK), F32)],
        compiler_params=_cparams(("parallel", "parallel", "arbitrary")),
        name="gla",
    )(p, p, p, a1, w2h, w2l, ba, mt, mk)


def _mix_kernel(yf_ref, o_ref, r_ref, m_ref, x_ref, gn_ref, wf_ref, wg_ref, wo_ref,
                g1_ref, n2g_ref, sc2_ref, sh2_ref, h_ref, n2t_ref):
    o = o_ref[0] + o_ref[1]
    parts = []
    for h in range(GLA_HEADS):
        oh = o[:, h * GLA_DV:(h + 1) * GLA_DV]
        ms = jnp.mean(oh * oh, axis=-1, keepdims=True)
        parts.append(oh * lax.rsqrt(ms + EPS * GLA_DK) * gn_ref[...])
    og = jnp.concatenate(parts, axis=1) * jax.nn.silu(r_ref[...].astype(F32))
    yg = _dot(og.astype(BF16), wg_ref[...])
    yf = _dot(yf_ref[...].astype(BF16), wf_ref[...])
    d = yf.shape[1]
    m = m_ref[...].astype(F32)
    mixed = jax.nn.sigmoid(m[:, :d]) * yf + jax.nn.sigmoid(m[:, d:]) * yg
    h1 = x_ref[...] + g1_ref[0] * _dot(mixed.astype(BF16), wo_ref[...])
    h_ref[...] = h1
    ms = jnp.mean(h1 * h1, axis=-1, keepdims=True)
    n2 = h1 * lax.rsqrt(ms + EPS) * n2g_ref[...]
    n2t_ref[...] = (n2 * (1.0 + sc2_ref[0]) + sh2_ref[0]).T.astype(BF16)


def _mix(yf, o, p, x2, gn, wf, wg, wo, mod3, n2g, seq, tm=256):
    t, d = x2.shape
    tm = min(tm, seq)
    bidx = lambda i: (i * tm) // seq
    const = lambda shape: pl.BlockSpec(shape, lambda i: (0,) * len(shape))
    roff = (F_WIDTH + 2 * GLA_QK + GLA_V) // GLA_V
    moff = (F_WIDTH + 2 * GLA_QK + 2 * GLA_V) // (2 * d)
    return pl.pallas_call(
        _mix_kernel,
        out_shape=(jax.ShapeDtypeStruct((t, d), F32), jax.ShapeDtypeStruct((d, t), BF16)),
        grid=(t // tm,),
        in_specs=[pl.BlockSpec((tm, F_WIDTH), lambda i: (i, 0)),
                  pl.BlockSpec((N_DIRS, tm, GLA_V), lambda i: (0, i, 0)),
                  pl.BlockSpec((tm, GLA_V), lambda i: (i, roff)),
                  pl.BlockSpec((tm, 2 * d), lambda i: (i, moff)),
                  pl.BlockSpec((tm, d), lambda i: (i, 0)),
                  const((1, GLA_DV)),
                  const((F_WIDTH, d)), const((GLA_V, d)), const((d, d)),
                  pl.BlockSpec((1, 1, d), lambda i: (bidx(i) * 6 + 2, 0, 0)),
                  const((1, d)),
                  pl.BlockSpec((1, 1, d), lambda i: (bidx(i) * 6 + 4, 0, 0)),
                  pl.BlockSpec((1, 1, d), lambda i: (bidx(i) * 6 + 3, 0, 0))],
        out_specs=(pl.BlockSpec((tm, d), lambda i: (i, 0)), pl.BlockSpec((d, tm), lambda i: (0, i))),
        compiler_params=_cparams(("parallel",)),
        name="mix",
    )(yf, o, p, p, x2, gn.reshape(1, GLA_DV), wf, wg, wo, mod3, n2g.reshape(1, d), mod3, mod3)


def _top_values(s, n):
    rows = []
    cur = s
    for r in range(n):
        mx = jnp.max(cur, axis=0, keepdims=True)
        rows.append(mx)
        if r + 1 < n:
            cur = jnp.where(cur == mx, -jnp.inf, cur)
    return jnp.concatenate(rows, axis=0)


def _pair_candidates(e1, e2):
    rows = [e1[0:1] * e2]
    for a in range(1, 8):
        rows.append(e1[a:a + 1] * e2[0:8])
    rows.append(e1[8:16] * e2[0:1])
    return jnp.concatenate(rows, axis=0)


def _peer_sel_kernel(n2t_ref, wqt_ref, keys_ref, e1_ref, e2_ref, th_ref):
    qt = _dot(wqt_ref[...], n2t_ref[...])
    for h in range(PEER_HEADS):
        base = h * 2 * PEER_HALF
        s1 = _dot(keys_ref[h, 0], qt[base:base + PEER_HALF].astype(BF16))
        s2 = _dot(keys_ref[h, 1], qt[base + PEER_HALF:base + 2 * PEER_HALF].astype(BF16))
        v1 = _top_values(s1, PEER_TOPK)
        v2 = _top_values(s2, PEER_TOPK)
        e1 = jnp.exp(v1 - v1[0:1])
        e2 = jnp.exp(v2 - v2[0:1])
        cand = _pair_candidates(e1, e2)
        cur = cand
        rem = jnp.full_like(cand[0:1], float(PEER_TOPK))
        theta = jnp.zeros_like(rem)
        for _ in range(PEER_TOPK):
            mx = jnp.max(cur, axis=0, keepdims=True)
            eq = cur == mx
            theta = jnp.where(rem > 0.0, mx, theta)
            rem = rem - jnp.sum(jnp.where(eq, 1.0, 0.0), axis=0, keepdims=True)
            cur = jnp.where(eq, -1.0, cur)
        sel = cand >= theta
        rz = 1.0 / jnp.sum(jnp.where(sel, cand, 0.0), axis=0, keepdims=True)
        candn = _pair_candidates(e1 * rz, e2)
        th_ref[h:h + 1, :] = jnp.min(jnp.where(sel, candn, jnp.inf), axis=0, keepdims=True)
        e1_ref[h] = jnp.exp(s1 - v1[0:1]) * rz
        e2_ref[h] = jnp.exp(s2 - v2[0:1])


def _peer_sel(n2t, wqt, keys, tm=256):
    d, t = n2t.shape
    tm = min(tm, t)
    nq = wqt.shape[0]
    return pl.pallas_call(
        _peer_sel_kernel,
        out_shape=(jax.ShapeDtypeStruct((PEER_HEADS, N_KEYS, t), F32),
                   jax.ShapeDtypeStruct((PEER_HEADS, N_KEYS, t), F32),
                   jax.ShapeDtypeStruct((PEER_HEADS, t), F32)),
        grid=(t // tm,),
        in_specs=[pl.BlockSpec((d, tm), lambda i: (0, i)),
                  pl.BlockSpec((nq, d), lambda i: (0, 0)),
                  pl.BlockSpec((PEER_HEADS, 2, N_KEYS, PEER_HALF), lambda i: (0, 0, 0, 0))],
        out_specs=(pl.BlockSpec((PEER_HEADS, N_KEYS, tm), lambda i: (0, 0, i)),
                   pl.BlockSpec((PEER_HEADS, N_KEYS, tm), lambda i: (0, 0, i)),
                   pl.BlockSpec((PEER_HEADS, tm), lambda i: (0, i))),
        compiler_params=_cparams(("parallel",)),
        name="peer_sel",
    )(n2t, wqt, keys)


def _peer_kernel(n2t_ref, e1_ref, e2_ref, th_ref, u_ref, vt_ref, out_ref,
                 a0_scr, a1_scr, w0_scr, w1_scr, *, n_e, ib, tt):
    k = pl.program_id(0)
    e_out = lax.rem(jnp.maximum(k - 2, 0), n_e)

    @pl.when(k == 0)
    def _():
        a1_scr[...] = jnp.zeros_like(a1_scr)
        w0_scr[...] = jnp.zeros_like(w0_scr)

    @pl.when((k == 0) | ((k >= 2) & (e_out == 0)))
    def _():
        out_ref[...] = jnp.zeros_like(out_ref)

    def gate_unit(a_prv, w_prv, ii, lc):
        rows = slice(ii * N_KEYS, (ii + 1) * N_KEYS)
        sl = slice(lc * LANES, (lc + 1) * LANES)
        g = jnp.zeros((N_KEYS, LANES), F32)
        for h in range(PEER_HEADS):
            p = e1_ref[h, ii:ii + 1, sl] * e2_ref[h, :, sl]
            g = g + jnp.where(p >= th_ref[h:h + 1, sl], p, 0.0)
        a = a_prv[rows, sl]
        act = 0.5 * a * (1.0 + lax.erf(a * (1.0 / math.sqrt(2.0))))
        w_prv[rows, sl] = (act * g).astype(BF16)

    def stages(a_cur, a_prv, w_cur, w_prv):
        eb, d = a_cur.shape[0], out_ref.shape[0]

        def pre_unit(r, c):
            rows, toks = slice(r * UNIT_ROWS, (r + 1) * UNIT_ROWS), slice(c * MXU_TILE, (c + 1) * MXU_TILE)
            a_cur[rows, toks] = _dot(u_ref[rows, :], n2t_ref[:, toks])

        def val_unit(r, c):
            rows, toks = slice(r * UNIT_ROWS, (r + 1) * UNIT_ROWS), slice(c * MXU_TILE, (c + 1) * MXU_TILE)
            out_ref[rows, toks] += _dot(vt_ref[0, rows, :], w_cur[:, toks])

        n_tc = tt // MXU_TILE
        pre = [functools.partial(pre_unit, r, c) for r in range(eb // UNIT_ROWS) for c in range(n_tc)]
        val = [functools.partial(val_unit, r, c) for r in range(d // UNIT_ROWS) for c in range(n_tc)]
        gate = [functools.partial(gate_unit, a_prv, w_prv, ii, lc) for ii in range(ib) for lc in range(tt // LANES)]
        for unit in _interleave(pre, val, gate):
            unit()

    @pl.when(lax.rem(k, 2) == 0)
    def _():
        stages(a0_scr, a1_scr, w0_scr, w1_scr)

    @pl.when(lax.rem(k, 2) == 1)
    def _():
        stages(a1_scr, a0_scr, w1_scr, w0_scr)


def _peer(n2t, e1, e2, th, u, v, seq, tt=1024, eb=1024):
    d, t = n2t.shape
    tt = min(tt, seq)
    n_e = u.shape[0] // eb
    vt = v.reshape(n_e, eb, d).transpose(0, 2, 1)
    n_blk = (t // tt) * n_e
    ib = eb // N_KEYS
    s1 = lambda k: jnp.minimum(k, n_blk - 1)
    s2 = lambda k: jnp.clip(k - 1, 0, n_blk - 1)
    s3 = lambda k: jnp.clip(k - 2, 0, n_blk - 1)
    once = pl.Buffered(1)
    return pl.pallas_call(
        functools.partial(_peer_kernel, n_e=n_e, ib=ib, tt=tt),
        out_shape=jax.ShapeDtypeStruct((d, t), F32),
        grid=(n_blk + 2,),
        in_specs=[pl.BlockSpec((d, tt), lambda k: (0, s1(k) // n_e), pipeline_mode=once),
                  pl.BlockSpec((PEER_HEADS, ib, tt), lambda k: (0, s2(k) % n_e, s2(k) // n_e)),
                  pl.BlockSpec((PEER_HEADS, N_KEYS, tt), lambda k: (0, 0, s2(k) // n_e), pipeline_mode=once),
                  pl.BlockSpec((PEER_HEADS, tt), lambda k: (0, s2(k) // n_e)),
                  pl.BlockSpec((eb, d), lambda k: (s1(k) % n_e, 0)),
                  pl.BlockSpec((1, d, eb), lambda k: (s3(k) % n_e, 0, 0))],
        out_specs=pl.BlockSpec((d, tt), lambda k: (0, s3(k) // n_e)),
        scratch_shapes=[pltpu.VMEM((eb, tt), F32), pltpu.VMEM((eb, tt), F32),
                        pltpu.VMEM((eb, tt), BF16), pltpu.VMEM((eb, tt), BF16)],
        compiler_params=_cparams(("arbitrary",)),
        name="peer",
    )(n2t, e1, e2, th, u, vt)


def _peer_out_kernel(pt_ref, h1_ref, g2_ref, fg_ref, out_ref, *, final_norm):
    h2 = h1_ref[...] + g2_ref[0] * pt_ref[...].T
    if final_norm:
        ms = jnp.mean(h2 * h2, axis=-1, keepdims=True)
        h2 = h2 * lax.rsqrt(ms + EPS) * fg_ref[...]
    out_ref[...] = h2


def _peer_out(pt, h1, mod3, fg, seq, final_norm, tm=512):
    t, d = h1.shape
    tm = min(tm, seq)
    bidx = lambda i: (i * tm) // seq
    return pl.pallas_call(
        functools.partial(_peer_out_kernel, final_norm=final_norm),
        out_shape=jax.ShapeDtypeStruct((t, d), F32),
        grid=(t // tm,),
        in_specs=[pl.BlockSpec((d, tm), lambda i: (0, i)),
                  pl.BlockSpec((tm, d), lambda i: (i, 0)),
                  pl.BlockSpec((1, 1, d), lambda i: (bidx(i) * 6 + 5, 0, 0)),
                  pl.BlockSpec((1, d), lambda i: (0, 0))],
        out_specs=pl.BlockSpec((tm, d), lambda i: (i, 0)),
        compiler_params=_cparams(("parallel",)),
        name="peer_out",
    )(pt, h1, mod3, fg.reshape(1, d))


def kernel(x, c, w_ada, b_ada, norm1_g, w_in, w_fnet, gla_w_a2, gla_b_a, gla_norm_g, w_gla, w_out,
           norm2_g, peer_w_q, peer_keys, peer_u, peer_v, final_norm_g):
    bsz, seq, d = x.shape
    t = bsz * seq
    depth = w_ada.shape[0]
    fc, ta, tc = _dft_tables(seq)
    mt, mk = _gla_tables(min(GLA_CHUNK, seq))
    n_main = F_WIDTH + 2 * GLA_QK + 2 * GLA_V
    h = x.reshape(t, d)
    for l in range(depth):
        mod3 = _ada(c, w_ada[l], b_ada[l]).reshape(bsz * 6, 1, d)
        w_main = jnp.concatenate([w_in[l][:, :n_main], w_in[l][:, n_main + N_DIRS * GATE_RANK:]], axis=1)
        w_a1 = jnp.zeros((d, LANES), F32).at[:, :N_DIRS * GATE_RANK].set(
            w_in[l][:, n_main:n_main + N_DIRS * GATE_RANK])
        p, a1 = _inproj(h, norm1_g[l], mod3, w_main.astype(BF16), w_a1.astype(BF16), seq)

        yf = _seq_dft(_chan_dft(p, fc), bsz, seq, ta, tc)

        w2 = jnp.zeros((N_DIRS, LANES, GLA_QK), F32)
        for dr in range(N_DIRS):
            w2 = w2.at[dr, dr * GATE_RANK:(dr + 1) * GATE_RANK].set(gla_w_a2[l][dr])
        w2h = w2.astype(BF16)
        w2l = (w2 - w2h.astype(F32)).astype(BF16)
        ba = gla_b_a[l].reshape(N_DIRS, 1, GLA_QK)
        o = _gla(p, a1, w2h, w2l, ba, mt, mk, bsz, seq)

        h1, n2t = _mix(yf, o, p, h, gla_norm_g[l], w_fnet[l].astype(BF16), w_gla[l].astype(BF16),
                      w_out[l].astype(BF16), mod3, norm2_g[l], seq)

        e1, e2, th = _peer_sel(n2t, peer_w_q[l].T.astype(BF16), peer_keys[l].astype(BF16))
        pt = _peer(n2t, e1, e2, th, peer_u[l].astype(BF16), peer_v[l].astype(BF16), seq)
        h = _peer_out(pt, h1, mod3, final_norm_g, seq, final_norm=(l == depth - 1))
    return h.reshape(bsz, seq, d)
```

```python
import functools
import math

import numpy as np
import jax
import jax.numpy as jnp
from jax import lax
from jax.experimental import pallas as pl
from jax.experimental.pallas import tpu as pltpu

BF16 = jnp.bfloat16
F32 = jnp.float32

EPS = 1e-6
F_GROUPS = 4
F_GROUP_DIM = 256
F_WIDTH = 1024
GLA_HEADS = 4
GLA_DK = 128
GLA_DV = 256
GLA_QK = 512
GLA_V = 1024
GATE_RANK = 16
GATE_TAU = 16.0
N_DIRS = 2
PEER_HEADS = 8
PEER_HALF = 128
N_KEYS = 128
PEER_TOPK = 16

LANES = 128
MXU_TILE = 256
UNIT_ROWS = 256
GLA_CHUNK = 128
FFT_N2 = 128
FFT_GROUP = 8
VMEM_LIMIT = 56 * 1024 * 1024

_NT = (((1,), (1,)), ((), ()))
_TN = (((0,), (0,)), ((), ()))


def _cparams(sem, flags=None):
    return pltpu.CompilerParams(dimension_semantics=sem, vmem_limit_bytes=VMEM_LIMIT, flags=flags)


def _dot(a, b):
    return jnp.dot(a, b, preferred_element_type=F32)


def _interleave(*streams):
    keyed = [((i + 0.5) / len(s), si, i, item) for si, s in enumerate(streams) for i, item in enumerate(s)]
    return [item for _, _, _, item in sorted(keyed, key=lambda t: t[:3])]


def _split_bf16(a):
    hi = a.astype(BF16)
    lo = (a - hi.astype(F32)).astype(BF16)
    return hi, lo


def _ada_kernel(c_ref, w_ref, b_ref, o_ref):
    s = jax.nn.silu(c_ref[...])
    o_ref[...] = _dot(s.astype(BF16), w_ref[...].astype(BF16)) + b_ref[...]


def _ada(c, w, b, tn=1536):
    bsz, d = c.shape
    n = w.shape[1]
    cp = jnp.zeros((8, d), F32).at[:bsz].set(c)
    out = pl.pallas_call(
        _ada_kernel,
        out_shape=jax.ShapeDtypeStruct((8, n), F32),
        grid=(n // tn,),
        in_specs=[pl.BlockSpec((8, d), lambda j: (0, 0)),
                  pl.BlockSpec((d, tn), lambda j: (0, j)),
                  pl.BlockSpec((1, tn), lambda j: (0, j))],
        out_specs=pl.BlockSpec((8, tn), lambda j: (0, j)),
        compiler_params=_cparams(("arbitrary",)),
        name="ada",
    )(cp, w, b.reshape(1, n))
    return out[:bsz]


def _inproj_kernel(x_ref, g_ref, sc_ref, sh_ref, w_ref, wa_ref, p_ref, a1_ref, n_scr):
    @pl.when(pl.program_id(1) == 0)
    def _():
        x = x_ref[...]
        ms = jnp.mean(x * x, axis=-1, keepdims=True)
        y = x * lax.rsqrt(ms + EPS) * g_ref[...]
        nb = (y * (1.0 + sc_ref[0]) + sh_ref[0]).astype(BF16)
        n_scr[...] = nb
        a1_ref[...] = _dot(nb, wa_ref[...])

    p_ref[...] = _dot(n_scr[...], w_ref[...]).astype(BF16)


def _inproj(x2, g, mod3, w, wa, seq, tm=1024, tn=1024):
    t, d = x2.shape
    n = w.shape[1]
    tm = min(tm, seq)
    bidx = lambda i: (i * tm) // seq
    return pl.pallas_call(
        _inproj_kernel,
        out_shape=(jax.ShapeDtypeStruct((t, n), BF16), jax.ShapeDtypeStruct((t, LANES), F32)),
        grid=(t // tm, n // tn),
        in_specs=[pl.BlockSpec((tm, d), lambda i, j: (i, 0)),
                  pl.BlockSpec((1, d), lambda i, j: (0, 0)),
                  pl.BlockSpec((1, 1, d), lambda i, j: (bidx(i) * 6 + 1, 0, 0)),
                  pl.BlockSpec((1, 1, d), lambda i, j: (bidx(i) * 6, 0, 0)),
                  pl.BlockSpec((d, tn), lambda i, j: (0, j)),
                  pl.BlockSpec((d, LANES), lambda i, j: (0, 0))],
        out_specs=(pl.BlockSpec((tm, tn), lambda i, j: (i, j)),
                   pl.BlockSpec((tm, LANES), lambda i, j: (i, 0))),
        scratch_shapes=[pltpu.VMEM((tm, d), BF16)],
        compiler_params=_cparams(("parallel", "arbitrary")),
        name="inproj",
    )(x2, g.reshape(1, d), mod3, mod3, w, wa)


def _chan_dft_kernel(z_ref, f_ref, w_ref):
    for g in range(F_GROUPS):
        lo, hi = g * F_GROUP_DIM, (g + 1) * F_GROUP_DIM
        r = _dot(z_ref[:, lo:hi], f_ref[...])
        w_ref[:, lo:hi] = r[:, :F_GROUP_DIM]
        w_ref[:, F_WIDTH + lo:F_WIDTH + hi] = r[:, F_GROUP_DIM:]


def _chan_dft(p, fc, tm=1024):
    t = p.shape[0]
    tm = min(tm, t)
    return pl.pallas_call(
        _chan_dft_kernel,
        out_shape=jax.ShapeDtypeStruct((t, 2 * F_WIDTH), F32),
        grid=(t // tm,),
        in_specs=[pl.BlockSpec((tm, F_WIDTH), lambda i: (i, 0)),
                  pl.BlockSpec((F_GROUP_DIM, 2 * F_GROUP_DIM), lambda i: (0, 0))],
        out_specs=pl.BlockSpec((tm, 2 * F_WIDTH), lambda i: (i, 0)),
        compiler_params=_cparams(("parallel",)),
        name="chan_dft",
    )(p, fc)


def _fft_a_kernel(w_ref, t_ref, x_ref):
    for j in range(FFT_GROUP):
        w = w_ref[0, :, j, :]
        rot = jnp.concatenate([w[:, F_WIDTH:], -w[:, :F_WIDTH]], axis=1)
        rhs = jnp.concatenate([w, rot], axis=0).astype(BF16)
        x_ref[0, j] = _dot(t_ref[j], rhs)


def _fft_c_kernel(x_ref, t_ref, y_ref):
    for j in range(FFT_GROUP):
        x = x_ref[0, :, j, :]
        rhs = jnp.concatenate([x[:, :F_WIDTH], x[:, F_WIDTH:]], axis=0).astype(BF16)
        y_ref[0, :, j, :] = _dot(t_ref[...], rhs)


def _seq_dft(wc, bsz, seq, ta, tc):
    n2 = FFT_N2
    n1 = seq // n2
    g = FFT_GROUP
    x1 = pl.pallas_call(
        _fft_a_kernel,
        out_shape=jax.ShapeDtypeStruct((bsz, n2, n1, 2 * F_WIDTH), F32),
        grid=(bsz, n2 // g),
        in_specs=[pl.BlockSpec((1, n1, g, 2 * F_WIDTH), lambda b, s: (b, 0, s, 0)),
                  pl.BlockSpec((g, n1, 2 * n1), lambda b, s: (s, 0, 0))],
        out_specs=pl.BlockSpec((1, g, n1, 2 * F_WIDTH), lambda b, s: (b, s, 0, 0)),
        compiler_params=_cparams(("parallel", "parallel")),
        name="fft_a",
    )(wc.reshape(bsz, n1, n2, 2 * F_WIDTH), ta)
    y = pl.pallas_call(
        _fft_c_kernel,
        out_shape=jax.ShapeDtypeStruct((bsz, n2, n1, F_WIDTH), F32),
        grid=(bsz, n1 // g),
        in_specs=[pl.BlockSpec((1, n2, g, 2 * F_WIDTH), lambda b, k: (b, 0, k, 0)),
                  pl.BlockSpec((n2, 2 * n2), lambda b, k: (0, 0))],
        out_specs=pl.BlockSpec((1, n2, g, F_WIDTH), lambda b, k: (b, 0, k, 0)),
        compiler_params=_cparams(("parallel", "parallel")),
        name="fft_c",
    )(x1, tc)
    return y.reshape(bsz * seq, F_WIDTH)


def _dft_tables(seq):
    n2 = FFT_N2
    n1 = seq // n2
    kc = np.arange(F_GROUP_DIM)
    ang = 2.0 * np.pi * ((kc[:, None] * kc[None, :]) % F_GROUP_DIM) / F_GROUP_DIM
    sc = 1.0 / math.sqrt(F_GROUP_DIM)
    fc = np.concatenate([np.cos(ang) * sc, -np.sin(ang) * sc], axis=1)
    k1 = np.arange(n1)[None, :, None]
    s1 = np.arange(n1)[None, None, :]
    s2 = np.arange(n2)[:, None, None]
    ang = 2.0 * np.pi * ((k1 * (s2 + n2 * s1)) % seq) / seq
    sa = 1.0 / math.sqrt(seq)
    ta = np.concatenate([np.cos(ang) * sa, np.sin(ang) * sa], axis=2)
    k2 = np.arange(n2)
    ang = 2.0 * np.pi * ((k2[:, None] * k2[None, :]) % n2) / n2
    tc = np.concatenate([np.cos(ang), np.sin(ang)], axis=1)
    as_bf16 = lambda a: jnp.asarray(a.astype(np.float32)).astype(BF16)
    return as_bf16(fc), as_bf16(ta), as_bf16(tc)


def _gla_levels(c):
    halves = []
    s = c // 2
    while s >= 1:
        halves.append(s)
        s //= 2
    return halves


def _gla_tables(c):
    r = np.arange(c)[:, None]
    l = np.arange(c)[None, :]
    mats = [(l <= r), (l > r)]
    masks = []
    for s in _gla_levels(c):
        m = (r // (2 * s)) * (2 * s) + s - 1
        q_side = r > m
        mats.append(np.where(q_side, (l > m) & (l <= r), (l > r) & (l <= m)))
        i, j = r, l
        masks.append((i // (2 * s) == j // (2 * s)) & (i % (2 * s) >= s) & (j % (2 * s) < s))
    masks.append(r == l)
    mt = np.stack(mats).astype(np.float32)
    mk = np.stack(masks).astype(np.float32)
    mt = np.stack([mt, mt[:, ::-1, ::-1]])
    mk = np.stack([mk, mk[:, ::-1, ::-1]])
    nt = mt.shape[1]
    return (jnp.asarray(mt.reshape(2, nt * c, c)).astype(BF16), jnp.asarray(mk))


def _gla_kernel(q_ref, k_ref, v_ref, a1_ref, w2h_ref, w2l_ref, ba_ref, mt_ref, mk_ref, o_ref, st_ref,
                *, chunk, n_lvl):
    c = chunk
    heads = range(GLA_HEADS)
    dk = lambda h: slice(h * GLA_DK, (h + 1) * GLA_DK)
    dv = lambda h: slice(h * GLA_DV, (h + 1) * GLA_DV)

    @pl.when(pl.program_id(2) == 0)
    def _():
        st_ref[...] = jnp.zeros_like(st_ref)

    a1h, a1l = _split_bf16(a1_ref[...])
    w2h = w2h_ref[0]
    z = _dot(a1h, w2h) + _dot(a1l, w2h) + _dot(a1h, w2l_ref[0]) + ba_ref[0]
    la = jax.nn.log_sigmoid(z) * (1.0 / GATE_TAU)
    lah, lal = _split_bf16(la)
    ex = _dot(mt_ref[0], jnp.concatenate([lah, lal], axis=1))
    ex = jnp.exp(ex[:, :GLA_QK] + ex[:, GLA_QK:])

    qb = q_ref[...]
    kb = k_ref[...]
    q = qb.astype(F32)
    k = kb.astype(F32)
    attn = [lax.dot_general(qb[:, dk(h)], kb[:, dk(h)], _NT, preferred_element_type=F32) * mk_ref[0, n_lvl]
            for h in heads]
    for lvl in range(n_lvl):
        e = ex[(2 + lvl) * c:(3 + lvl) * c]
        qe = (q * e).astype(BF16)
        ke = (k * e).astype(BF16)
        for h in heads:
            a = lax.dot_general(qe[:, dk(h)], ke[:, dk(h)], _NT, preferred_element_type=F32)
            attn[h] = attn[h] + a * mk_ref[0, lvl]
    e_cum = ex[0:c]
    e_rev = ex[c:2 * c]
    qc = (q * e_cum).astype(BF16)
    kd = (k * e_rev).astype(BF16)
    tot = e_cum[0:1] * e_rev[0:1]
    for h in heads:
        v = v_ref[:, dv(h)]
        st = st_ref[h]
        o = _dot(attn[h].astype(BF16), v)
        o = o + lax.dot_general(qc[:, dk(h)], st.astype(BF16), _NT, preferred_element_type=F32)
        o_ref[0, :, dv(h)] = o
        st_ref[h] = st * tot[:, dk(h)] + lax.dot_general(v, kd[:, dk(h)], _TN, preferred_element_type=F32)


def _gla(p, a1, w2h, w2l, ba, mt, mk, bsz, seq):
    t = p.shape[0]
    c = min(GLA_CHUNK, seq)
    nc = seq // c
    n_lvl = len(_gla_levels(c))
    row = lambda d, b, c_: b * nc + c_ + d * (nc - 1 - 2 * c_)
    qoff = F_WIDTH // GLA_QK
    koff = (F_WIDTH + GLA_QK) // GLA_QK
    voff = (F_WIDTH + 2 * GLA_QK) // GLA_V
    return pl.pallas_call(
        functools.partial(_gla_kernel, chunk=c, n_lvl=n_lvl),
        out_shape=jax.ShapeDtypeStruct((N_DIRS, t, GLA_V), F32),
        grid=(N_DIRS, bsz, nc),
        in_specs=[pl.BlockSpec((c, GLA_QK), lambda d, b, c_: (row(d, b, c_), qoff)),
                  pl.BlockSpec((c, GLA_QK), lambda d, b, c_: (row(d, b, c_), koff)),
                  pl.BlockSpec((c, GLA_V), lambda d, b, c_: (row(d, b, c_), voff)),
                  pl.BlockSpec((c, LANES), lambda d, b, c_: (row(d, b, c_), 0)),
                  pl.BlockSpec((1, LANES, GLA_QK), lambda d, b, c_: (d, 0, 0)),
                  pl.BlockSpec((1, LANES, GLA_QK), lambda d, b, c_: (d, 0, 0)),
                  pl.BlockSpec((1, 1, GLA_QK), lambda d, b, c_: (d, 0, 0)),
                  pl.BlockSpec((1,) + mt.shape[1:], lambda d, b, c_: (d, 0, 0)),
                  pl.BlockSpec((1,) + mk.shape[1:], lambda d, b, c_: (d, 0, 0, 0))],
        out_specs=pl.BlockSpec((1, c, GLA_V), lambda d, b, c_: (d, row(d, b, c_), 0)),
        scratch_shapes=[pltpu.VMEM((GLA_HEADS, GLA_DV, GLA_DK), F32)],
        compiler_params=_cparams(("parallel", "parallel", "arbitrary")),
        name="gla",
    )(p, p, p, a1, w2h, w2l, ba, mt, mk)


def _mix_kernel(yf_ref, o_ref, r_ref, m_ref, x_ref, gn_ref, wf_ref, wg_ref, wo_ref,
                g1_ref, n2g_ref, sc2_ref, sh2_ref, h_ref, n2t_ref):
    o = o_ref[0] + o_ref[1]
    parts = []
    for h in range(GLA_HEADS):
        oh = o[:, h * GLA_DV:(h + 1) * GLA_DV]
        ms = jnp.mean(oh * oh, axis=-1, keepdims=True)
        parts.append(oh * lax.rsqrt(ms + EPS * GLA_DK) * gn_ref[...])
    og = jnp.concatenate(parts, axis=1) * jax.nn.silu(r_ref[...].astype(F32))
    yg = _dot(og.astype(BF16), wg_ref[...])
    yf = _dot(yf_ref[...].astype(BF16), wf_ref[...])
    d = yf.shape[1]
    m = m_ref[...].astype(F32)
    mixed = jax.nn.sigmoid(m[:, :d]) * yf + jax.nn.sigmoid(m[:, d:]) * yg
    h1 = x_ref[...] + g1_ref[0] * _dot(mixed.astype(BF16), wo_ref[...])
    h_ref[...] = h1
    ms = jnp.mean(h1 * h1, axis=-1, keepdims=True)
    n2 = h1 * lax.rsqrt(ms + EPS) * n2g_ref[...]
    n2t_ref[...] = (n2 * (1.0 + sc2_ref[0]) + sh2_ref[0]).T.astype(BF16)


def _mix(yf, o, p, x2, gn, wf, wg, wo, mod3, n2g, seq, tm=256):
    t, d = x2.shape
    tm = min(tm, seq)
    bidx = lambda i: (i * tm) // seq
    const = lambda shape: pl.BlockSpec(shape, lambda i: (0,) * len(shape))
    roff = (F_WIDTH + 2 * GLA_QK + GLA_V) // GLA_V
    moff = (F_WIDTH + 2 * GLA_QK + 2 * GLA_V) // (2 * d)
    return pl.pallas_call(
        _mix_kernel,
        out_shape=(jax.ShapeDtypeStruct((t, d), F32), jax.ShapeDtypeStruct((d, t), BF16)),
        grid=(t // tm,),
        in_specs=[pl.BlockSpec((tm, F_WIDTH), lambda i: (i, 0)),
                  pl.BlockSpec((N_DIRS, tm, GLA_V), lambda i: (0, i, 0)),
                  pl.BlockSpec((tm, GLA_V), lambda i: (i, roff)),
                  pl.BlockSpec((tm, 2 * d), lambda i: (i, moff)),
                  pl.BlockSpec((tm, d), lambda i: (i, 0)),
                  const((1, GLA_DV)),
                  const((F_WIDTH, d)), const((GLA_V, d)), const((d, d)),
                  pl.BlockSpec((1, 1, d), lambda i: (bidx(i) * 6 + 2, 0, 0)),
                  const((1, d)),
                  pl.BlockSpec((1, 1, d), lambda i: (bidx(i) * 6 + 4, 0, 0)),
                  pl.BlockSpec((1, 1, d), lambda i: (bidx(i) * 6 + 3, 0, 0))],
        out_specs=(pl.BlockSpec((tm, d), lambda i: (i, 0)), pl.BlockSpec((d, tm), lambda i: (0, i))),
        compiler_params=_cparams(("parallel",)),
        name="mix",
    )(yf, o, p, p, x2, gn.reshape(1, GLA_DV), wf, wg, wo, mod3, n2g.reshape(1, d), mod3, mod3)


def _top_values(s, n):
    rows = []
    cur = s
    for r in range(n):
        mx = jnp.max(cur, axis=0, keepdims=True)
        rows.append(mx)
        if r + 1 < n:
            cur = jnp.where(cur == mx, -jnp.inf, cur)
    return jnp.concatenate(rows, axis=0)


def _pair_candidates(e1, e2):
    rows = [e1[0:1] * e2]
    for a in range(1, 8):
        rows.append(e1[a:a + 1] * e2[0:8])
    rows.append(e1[8:16] * e2[0:1])
    return jnp.concatenate(rows, axis=0)


def _peer_sel_kernel(n2t_ref, wqt_ref, keys_ref, e1_ref, e2_ref, th_ref):
    qt = _dot(wqt_ref[...], n2t_ref[...])
    for h in range(PEER_HEADS):
        base = h * 2 * PEER_HALF
        s1 = _dot(keys_ref[h, 0], qt[base:base + PEER_HALF].astype(BF16))
        s2 = _dot(keys_ref[h, 1], qt[base + PEER_HALF:base + 2 * PEER_HALF].astype(BF16))
        v1 = _top_values(s1, PEER_TOPK)
        v2 = _top_values(s2, PEER_TOPK)
        e1 = jnp.exp(v1 - v1[0:1])
        e2 = jnp.exp(v2 - v2[0:1])
        cand = _pair_candidates(e1, e2)
        cur = cand
        rem = jnp.full_like(cand[0:1], float(PEER_TOPK))
        theta = jnp.zeros_like(rem)
        for _ in range(PEER_TOPK):
            mx = jnp.max(cur, axis=0, keepdims=True)
            eq = cur == mx
            theta = jnp.where(rem > 0.0, mx, theta)
            rem = rem - jnp.sum(jnp.where(eq, 1.0, 0.0), axis=0, keepdims=True)
            cur = jnp.where(eq, -1.0, cur)
        sel = cand >= theta
        rz = 1.0 / jnp.sum(jnp.where(sel, cand, 0.0), axis=0, keepdims=True)
        candn = _pair_candidates(e1 * rz, e2)
        th_ref[h:h + 1, :] = jnp.min(jnp.where(sel, candn, jnp.inf), axis=0, keepdims=True)
        e1_ref[h] = jnp.exp(s1 - v1[0:1]) * rz
        e2_ref[h] = jnp.exp(s2 - v2[0:1])


def _peer_sel(n2t, wqt, keys, tm=256):
    d, t = n2t.shape
    tm = min(tm, t)
    nq = wqt.shape[0]
    return pl.pallas_call(
        _peer_sel_kernel,
        out_shape=(jax.ShapeDtypeStruct((PEER_HEADS, N_KEYS, t), F32),
                   jax.ShapeDtypeStruct((PEER_HEADS, N_KEYS, t), F32),
                   jax.ShapeDtypeStruct((PEER_HEADS, t), F32)),
        grid=(t // tm,),
        in_specs=[pl.BlockSpec((d, tm), lambda i: (0, i)),
                  pl.BlockSpec((nq, d), lambda i: (0, 0)),
                  pl.BlockSpec((PEER_HEADS, 2, N_KEYS, PEER_HALF), lambda i: (0, 0, 0, 0))],
        out_specs=(pl.BlockSpec((PEER_HEADS, N_KEYS, tm), lambda i: (0, 0, i)),
                   pl.BlockSpec((PEER_HEADS, N_KEYS, tm), lambda i: (0, 0, i)),
                   pl.BlockSpec((PEER_HEADS, tm), lambda i: (0, i))),
        compiler_params=_cparams(("parallel",)),
        name="peer_sel",
    )(n2t, wqt, keys)


def _peer_kernel(n2t_ref, e1_ref, e2_ref, th_ref, u_ref, vt_ref, out_ref,
                 a_scr, w_scr, *, n_e, ib, tt):
    k = pl.program_id(0)
    e_out = lax.rem(jnp.maximum(k - 2, 0), n_e)

    @pl.when(k == 0)
    def _():
        a_scr[1] = jnp.zeros(a_scr.shape[1:], F32)
        w_scr[0] = jnp.zeros(w_scr.shape[1:], BF16)

    @pl.when((k == 0) | ((k >= 2) & (e_out == 0)))
    def _():
        out_ref[...] = jnp.zeros_like(out_ref)

    def gate_unit(a_prv, w_prv, ii, lc):
        rows = slice(ii * N_KEYS, (ii + 1) * N_KEYS)
        sl = slice(lc * LANES, (lc + 1) * LANES)
        g = jnp.zeros((N_KEYS, LANES), F32)
        for h in range(PEER_HEADS):
            p = e1_ref[h, ii:ii + 1, sl] * e2_ref[h, :, sl]
            g = g + jnp.where(p >= th_ref[h:h + 1, sl], p, 0.0)
        a = a_prv[rows, sl]
        act = 0.5 * a * (1.0 + lax.erf(a * (1.0 / math.sqrt(2.0))))
        w_prv[rows, sl] = (act * g).astype(BF16)

    def stages(a_cur, a_prv, w_cur, w_prv):
        eb, d = a_cur.shape[0], out_ref.shape[0]

        def pre_unit(r, c):
            rows, toks = slice(r * UNIT_ROWS, (r + 1) * UNIT_ROWS), slice(c * MXU_TILE, (c + 1) * MXU_TILE)
            a_cur[rows, toks] = _dot(u_ref[rows, :], n2t_ref[:, toks])

        def val_unit(r, c):
            rows, toks = slice(r * UNIT_ROWS, (r + 1) * UNIT_ROWS), slice(c * MXU_TILE, (c + 1) * MXU_TILE)
            out_ref[rows, toks] += _dot(vt_ref[0, rows, :], w_cur[:, toks])

        n_tc = tt // MXU_TILE
        pre = [functools.partial(pre_unit, r, c) for r in range(eb // UNIT_ROWS) for c in range(n_tc)]
        val = [functools.partial(val_unit, r, c) for r in range(d // UNIT_ROWS) for c in range(n_tc)]
        gate = [functools.partial(gate_unit, a_prv, w_prv, ii, lc) for ii in range(ib) for lc in range(tt // LANES)]
        for unit in _interleave(pre, val, gate):
            unit()

    cur = lax.rem(k, 2)
    prv = 1 - cur
    stages(a_scr.at[cur], a_scr.at[prv], w_scr.at[cur], w_scr.at[prv])


def _peer(n2t, e1, e2, th, u, v, seq, tt=1024, eb=1024):
    d, t = n2t.shape
    tt = min(tt, seq)
    n_e = u.shape[0] // eb
    vt = v.reshape(n_e, eb, d).transpose(0, 2, 1)
    n_blk = (t // tt) * n_e
    ib = eb // N_KEYS
    s1 = lambda k: jnp.minimum(k, n_blk - 1)
    s2 = lambda k: jnp.clip(k - 1, 0, n_blk - 1)
    s3 = lambda k: jnp.clip(k - 2, 0, n_blk - 1)
    once = pl.Buffered(1)
    return pl.pallas_call(
        functools.partial(_peer_kernel, n_e=n_e, ib=ib, tt=tt),
        out_shape=jax.ShapeDtypeStruct((d, t), F32),
        grid=(n_blk + 2,),
        in_specs=[pl.BlockSpec((d, tt), lambda k: (0, s1(k) // n_e), pipeline_mode=once),
                  pl.BlockSpec((PEER_HEADS, ib, tt), lambda k: (0, s2(k) % n_e, s2(k) // n_e)),
                  pl.BlockSpec((PEER_HEADS, N_KEYS, tt), lambda k: (0, 0, s2(k) // n_e), pipeline_mode=once),
                  pl.BlockSpec((PEER_HEADS, tt), lambda k: (0, s2(k) // n_e)),
                  pl.BlockSpec((eb, d), lambda k: (s1(k) % n_e, 0)),
                  pl.BlockSpec((1, d, eb), lambda k: (s3(k) % n_e, 0, 0))],
        out_specs=pl.BlockSpec((d, tt), lambda k: (0, s3(k) // n_e)),
        scratch_shapes=[pltpu.VMEM((2, eb, tt), F32), pltpu.VMEM((2, eb, tt), BF16)],
        compiler_params=_cparams(("arbitrary",)),
        name="peer",
    )(n2t, e1, e2, th, u, vt)


def _peer_out_kernel(pt_ref, h1_ref, g2_ref, fg_ref, out_ref, *, final_norm):
    h2 = h1_ref[...] + g2_ref[0] * pt_ref[...].T
    if final_norm:
        ms = jnp.mean(h2 * h2, axis=-1, keepdims=True)
        h2 = h2 * lax.rsqrt(ms + EPS) * fg_ref[...]
    out_ref[...] = h2


def _peer_out(pt, h1, mod3, fg, seq, final_norm, tm=512):
    t, d = h1.shape
    tm = min(tm, seq)
    bidx = lambda i: (i * tm) // seq
    return pl.pallas_call(
        functools.partial(_peer_out_kernel, final_norm=final_norm),
        out_shape=jax.ShapeDtypeStruct((t, d), F32),
        grid=(t // tm,),
        in_specs=[pl.BlockSpec((d, tm), lambda i: (0, i)),
                  pl.BlockSpec((tm, d), lambda i: (i, 0)),
                  pl.BlockSpec((1, 1, d), lambda i: (bidx(i) * 6 + 5, 0, 0)),
                  pl.BlockSpec((1, d), lambda i: (0, 0))],
        out_specs=pl.BlockSpec((tm, d), lambda i: (i, 0)),
        compiler_params=_cparams(("parallel",)),
        name="peer_out",
    )(pt, h1, mod3, fg.reshape(1, d))


def kernel(x, c, w_ada, b_ada, norm1_g, w_in, w_fnet, gla_w_a2, gla_b_a, gla_norm_g, w_gla, w_out,
           norm2_g, peer_w_q, peer_keys, peer_u, peer_v, final_norm_g):
    bsz, seq, d = x.shape
    t = bsz * seq
    depth = w_ada.shape[0]
    fc, ta, tc = _dft_tables(seq)
    mt, mk = _gla_tables(min(GLA_CHUNK, seq))
    n_main = F_WIDTH + 2 * GLA_QK + 2 * GLA_V
    h = x.reshape(t, d)
    for l in range(depth):
        mod3 = _ada(c, w_ada[l], b_ada[l]).reshape(bsz * 6, 1, d)
        w_main = jnp.concatenate([w_in[l][:, :n_main], w_in[l][:, n_main + N_DIRS * GATE_RANK:]], axis=1)
        w_a1 = jnp.zeros((d, LANES), F32).at[:, :N_DIRS * GATE_RANK].set(
            w_in[l][:, n_main:n_main + N_DIRS * GATE_RANK])
        p, a1 = _inproj(h, norm1_g[l], mod3, w_main.astype(BF16), w_a1.astype(BF16), seq)

        yf = _seq_dft(_chan_dft(p, fc), bsz, seq, ta, tc)

        w2 = jnp.zeros((N_DIRS, LANES, GLA_QK), F32)
        for dr in range(N_DIRS):
            w2 = w2.at[dr, dr * GATE_RANK:(dr + 1) * GATE_RANK].set(gla_w_a2[l][dr])
        w2h = w2.astype(BF16)
        w2l = (w2 - w2h.astype(F32)).astype(BF16)
        ba = gla_b_a[l].reshape(N_DIRS, 1, GLA_QK)
        o = _gla(p, a1, w2h, w2l, ba, mt, mk, bsz, seq)

        h1, n2t = _mix(yf, o, p, h, gla_norm_g[l], w_fnet[l].astype(BF16), w_gla[l].astype(BF16),
                      w_out[l].astype(BF16), mod3, norm2_g[l], seq)

        e1, e2, th = _peer_sel(n2t, peer_w_q[l].T.astype(BF16), peer_keys[l].astype(BF16))
        pt = _peer(n2t, e1, e2, th, peer_u[l].astype(BF16), peer_v[l].astype(BF16), seq)
        h = _peer_out(pt, h1, mod3, final_norm_g, seq, final_norm=(l == depth - 1))
    return h.reshape(bsz, seq, d)
```

```python
import functools
import math

import numpy as np
import jax
import jax.numpy as jnp
from jax import lax
from jax.experimental import pallas as pl
from jax.experimental.pallas import tpu as pltpu

BF16 = jnp.bfloat16
F32 = jnp.float32

EPS = 1e-6
F_GROUPS = 4
F_GROUP_DIM = 256
F_WIDTH = 1024
GLA_HEADS = 4
GLA_DK = 128
GLA_DV = 256
GLA_QK = 512
GLA_V = 1024
GATE_RANK = 16
GATE_TAU = 16.0
N_DIRS = 2
PEER_HEADS = 8
PEER_HALF = 128
N_KEYS = 128
PEER_TOPK = 16

LANES = 128
MXU_TILE = 256
UNIT_ROWS = 256
GLA_CHUNK = 128
FFT_N2 = 128
FFT_GROUP = 8
VMEM_LIMIT = 56 * 1024 * 1024

_NT = (((1,), (1,)), ((), ()))
_TN = (((0,), (0,)), ((), ()))


def _cparams(sem, flags=None):
    return pltpu.CompilerParams(dimension_semantics=sem, vmem_limit_bytes=VMEM_LIMIT, flags=flags)


def _dot(a, b):
    return jnp.dot(a, b, preferred_element_type=F32)


def _interleave(*streams):
    keyed = [((i + 0.5) / len(s), si, i, item) for si, s in enumerate(streams) for i, item in enumerate(s)]
    return [item for _, _, _, item in sorted(keyed, key=lambda t: t[:3])]


def _split_bf16(a):
    hi = a.astype(BF16)
    lo = (a - hi.astype(F32)).astype(BF16)
    return hi, lo


def _ada_kernel(c_ref, w_ref, b_ref, o_ref):
    s = jax.nn.silu(c_ref[...])
    o_ref[...] = _dot(s.astype(BF16), w_ref[...].astype(BF16)) + b_ref[...]


def _ada(c, w, b, tn=1536):
    bsz, d = c.shape
    n = w.shape[1]
    cp = jnp.zeros((8, d), F32).at[:bsz].set(c)
    out = pl.pallas_call(
        _ada_kernel,
        out_shape=jax.ShapeDtypeStruct((8, n), F32),
        grid=(n // tn,),
        in_specs=[pl.BlockSpec((8, d), lambda j: (0, 0)),
                  pl.BlockSpec((d, tn), lambda j: (0, j)),
                  pl.BlockSpec((1, tn), lambda j: (0, j))],
        out_specs=pl.BlockSpec((8, tn), lambda j: (0, j)),
        compiler_params=_cparams(("arbitrary",)),
        name="ada",
    )(cp, w, b.reshape(1, n))
    return out[:bsz]


def _inproj_kernel(x_ref, g_ref, sc_ref, sh_ref, w_ref, wa_ref, p_ref, a1_ref, n_scr):
    @pl.when(pl.program_id(1) == 0)
    def _():
        x = x_ref[...]
        ms = jnp.mean(x * x, axis=-1, keepdims=True)
        y = x * lax.rsqrt(ms + EPS) * g_ref[...]
        nb = (y * (1.0 + sc_ref[0]) + sh_ref[0]).astype(BF16)
        n_scr[...] = nb
        a1_ref[...] = _dot(nb, wa_ref[...])

    p_ref[...] = _dot(n_scr[...], w_ref[...]).astype(BF16)


def _inproj(x2, g, mod3, w, wa, seq, tm=1024, tn=1024):
    t, d = x2.shape
    n = w.shape[1]
    tm = min(tm, seq)
    bidx = lambda i: (i * tm) // seq
    return pl.pallas_call(
        _inproj_kernel,
        out_shape=(jax.ShapeDtypeStruct((t, n), BF16), jax.ShapeDtypeStruct((t, LANES), F32)),
        grid=(t // tm, n // tn),
        in_specs=[pl.BlockSpec((tm, d), lambda i, j: (i, 0)),
                  pl.BlockSpec((1, d), lambda i, j: (0, 0)),
                  pl.BlockSpec((1, 1, d), lambda i, j: (bidx(i) * 6 + 1, 0, 0)),
                  pl.BlockSpec((1, 1, d), lambda i, j: (bidx(i) * 6, 0, 0)),
                  pl.BlockSpec((d, tn), lambda i, j: (0, j)),
                  pl.BlockSpec((d, LANES), lambda i, j: (0, 0))],
        out_specs=(pl.BlockSpec((tm, tn), lambda i, j: (i, j)),
                   pl.BlockSpec((tm, LANES), lambda i, j: (i, 0))),
        scratch_shapes=[pltpu.VMEM((tm, d), BF16)],
        compiler_params=_cparams(("parallel", "arbitrary")),
        name="inproj",
    )(x2, g.reshape(1, d), mod3, mod3, w, wa)


def _chan_dft_kernel(z_ref, f_ref, w_ref):
    for g in range(F_GROUPS):
        lo, hi = g * F_GROUP_DIM, (g + 1) * F_GROUP_DIM
        r = _dot(z_ref[:, lo:hi], f_ref[...])
        w_ref[:, lo:hi] = r[:, :F_GROUP_DIM]
        w_ref[:, F_WIDTH + lo:F_WIDTH + hi] = r[:, F_GROUP_DIM:]


def _chan_dft(p, fc, tm=1024):
    t = p.shape[0]
    tm = min(tm, t)
    return pl.pallas_call(
        _chan_dft_kernel,
        out_shape=jax.ShapeDtypeStruct((t, 2 * F_WIDTH), F32),
        grid=(t // tm,),
        in_specs=[pl.BlockSpec((tm, F_WIDTH), lambda i: (i, 0)),
                  pl.BlockSpec((F_GROUP_DIM, 2 * F_GROUP_DIM), lambda i: (0, 0))],
        out_specs=pl.BlockSpec((tm, 2 * F_WIDTH), lambda i: (i, 0)),
        compiler_params=_cparams(("parallel",)),
        name="chan_dft",
    )(p, fc)


def _fft_a_kernel(w_ref, t_ref, x_ref):
    for j in range(FFT_GROUP):
        w = w_ref[0, :, j, :]
        rot = jnp.concatenate([w[:, F_WIDTH:], -w[:, :F_WIDTH]], axis=1)
        rhs = jnp.concatenate([w, rot], axis=0).astype(BF16)
        x_ref[0, j] = _dot(t_ref[j], rhs)


def _fft_c_kernel(x_ref, t_ref, y_ref):
    for j in range(FFT_GROUP):
        x = x_ref[0, :, j, :]
        rhs = jnp.concatenate([x[:, :F_WIDTH], x[:, F_WIDTH:]], axis=0).astype(BF16)
        y_ref[0, :, j, :] = _dot(t_ref[...], rhs)


def _seq_dft(wc, bsz, seq, ta, tc):
    n2 = FFT_N2
    n1 = seq // n2
    g = FFT_GROUP
    x1 = pl.pallas_call(
        _fft_a_kernel,
        out_shape=jax.ShapeDtypeStruct((bsz, n2, n1, 2 * F_WIDTH), F32),
        grid=(bsz, n2 // g),
        in_specs=[pl.BlockSpec((1, n1, g, 2 * F_WIDTH), lambda b, s: (b, 0, s, 0)),
                  pl.BlockSpec((g, n1, 2 * n1), lambda b, s: (s, 0, 0))],
        out_specs=pl.BlockSpec((1, g, n1, 2 * F_WIDTH), lambda b, s: (b, s, 0, 0)),
        compiler_params=_cparams(("parallel", "parallel")),
        name="fft_a",
    )(wc.reshape(bsz, n1, n2, 2 * F_WIDTH), ta)
    y = pl.pallas_call(
        _fft_c_kernel,
        out_shape=jax.ShapeDtypeStruct((bsz, n2, n1, F_WIDTH), F32),
        grid=(bsz, n1 // g),
        in_specs=[pl.BlockSpec((1, n2, g, 2 * F_WIDTH), lambda b, k: (b, 0, k, 0)),
                  pl.BlockSpec((n2, 2 * n2), lambda b, k: (0, 0))],
        out_specs=pl.BlockSpec((1, n2, g, F_WIDTH), lambda b, k: (b, 0, k, 0)),
        compiler_params=_cparams(("parallel", "parallel")),
        name="fft_c",
    )(x1, tc)
    return y.reshape(bsz * seq, F_WIDTH)


def _dft_tables(seq):
    n2 = FFT_N2
    n1 = seq // n2
    kc = np.arange(F_GROUP_DIM)
    ang = 2.0 * np.pi * ((kc[:, None] * kc[None, :]) % F_GROUP_DIM) / F_GROUP_DIM
    sc = 1.0 / math.sqrt(F_GROUP_DIM)
    fc = np.concatenate([np.cos(ang) * sc, -np.sin(ang) * sc], axis=1)
    k1 = np.arange(n1)[None, :, None]
    s1 = np.arange(n1)[None, None, :]
    s2 = np.arange(n2)[:, None, None]
    ang = 2.0 * np.pi * ((k1 * (s2 + n2 * s1)) % seq) / seq
    sa = 1.0 / math.sqrt(seq)
    ta = np.concatenate([np.cos(ang) * sa, np.sin(ang) * sa], axis=2)
    k2 = np.arange(n2)
    ang = 2.0 * np.pi * ((k2[:, None] * k2[None, :]) % n2) / n2
    tc = np.concatenate([np.cos(ang), np.sin(ang)], axis=1)
    as_bf16 = lambda a: jnp.asarray(a.astype(np.float32)).astype(BF16)
    return as_bf16(fc), as_bf16(ta), as_bf16(tc)


def _gla_levels(c):
    halves = []
    s = c // 2
    while s >= 1:
        halves.append(s)
        s //= 2
    return halves


def _gla_tables(c):
    r = np.arange(c)[:, None]
    l = np.arange(c)[None, :]
    mats = [(l <= r), (l > r)]
    masks = []
    for s in _gla_levels(c):
        m = (r // (2 * s)) * (2 * s) + s - 1
        q_side = r > m
        mats.append(np.where(q_side, (l > m) & (l <= r), (l > r) & (l <= m)))
        i, j = r, l
        masks.append((i // (2 * s) == j // (2 * s)) & (i % (2 * s) >= s) & (j % (2 * s) < s))
    masks.append(r == l)
    mt = np.stack(mats).astype(np.float32)
    mk = np.stack(masks).astype(np.float32)
    mt = np.stack([mt, mt[:, ::-1, ::-1]])
    mk = np.stack([mk, mk[:, ::-1, ::-1]])
    nt = mt.shape[1]
    return (jnp.asarray(mt.reshape(2, nt * c, c)).astype(BF16), jnp.asarray(mk))


def _gla_kernel(qf_ref, kf_ref, vf_ref, af_ref, qb_ref, kb_ref, vb_ref, ab_ref,
                w2h_ref, w2l_ref, ba_ref, mt_ref, mk_ref, of_ref, ob_ref, st_ref, *, chunk, n_lvl):
    c = chunk
    dirs = range(N_DIRS)
    heads = range(GLA_HEADS)
    dk = lambda h: slice(h * GLA_DK, (h + 1) * GLA_DK)
    dv = lambda h: slice(h * GLA_DV, (h + 1) * GLA_DV)
    q_refs, k_refs, v_refs = (qf_ref, qb_ref), (kf_ref, kb_ref), (vf_ref, vb_ref)
    a_refs, o_refs = (af_ref, ab_ref), (of_ref, ob_ref)

    @pl.when(pl.program_id(1) == 0)
    def _():
        st_ref[...] = jnp.zeros_like(st_ref)

    ex = []
    for d in dirs:
        a1h, a1l = _split_bf16(a_refs[d][...])
        w2h = w2h_ref[d]
        z = _dot(a1h, w2h) + _dot(a1l, w2h) + _dot(a1h, w2l_ref[d]) + ba_ref[d]
        la = jax.nn.log_sigmoid(z) * (1.0 / GATE_TAU)
        lah, lal = _split_bf16(la)
        e = _dot(mt_ref[d], jnp.concatenate([lah, lal], axis=1))
        ex.append(jnp.exp(e[:, :GLA_QK] + e[:, GLA_QK:]))

    qb = [q_refs[d][...] for d in dirs]
    kb = [k_refs[d][...] for d in dirs]
    q = [x.astype(F32) for x in qb]
    k = [x.astype(F32) for x in kb]
    attn = [[lax.dot_general(qb[d][:, dk(h)], kb[d][:, dk(h)], _NT, preferred_element_type=F32)
             * mk_ref[d, n_lvl] for h in heads] for d in dirs]
    for lvl in range(n_lvl):
        for d in dirs:
            e = ex[d][(2 + lvl) * c:(3 + lvl) * c]
            qe = (q[d] * e).astype(BF16)
            ke = (k[d] * e).astype(BF16)
            for h in heads:
                a = lax.dot_general(qe[:, dk(h)], ke[:, dk(h)], _NT, preferred_element_type=F32)
                attn[d][h] = attn[d][h] + a * mk_ref[d, lvl]
    for d in dirs:
        e_cum = ex[d][0:c]
        e_rev = ex[d][c:2 * c]
        qc = (q[d] * e_cum).astype(BF16)
        kd = (k[d] * e_rev).astype(BF16)
        tot = e_cum[0:1] * e_rev[0:1]
        for h in heads:
            v = v_refs[d][:, dv(h)]
            st = st_ref[d, h]
            o = _dot(attn[d][h].astype(BF16), v)
            o = o + lax.dot_general(qc[:, dk(h)], st.astype(BF16), _NT, preferred_element_type=F32)
            o_refs[d][:, dv(h)] = o.astype(BF16)
            st_ref[d, h] = st * tot[:, dk(h)] + lax.dot_general(v, kd[:, dk(h)], _TN,
                                                                preferred_element_type=F32)


def _gla(p, a1, w2h, w2l, ba, mt, mk, bsz, seq):
    t = p.shape[0]
    c = min(GLA_CHUNK, seq)
    nc = seq // c
    n_lvl = len(_gla_levels(c))
    fwd = lambda b, c_: b * nc + c_
    bwd = lambda b, c_: b * nc + nc - 1 - c_
    qoff = F_WIDTH // GLA_QK
    koff = (F_WIDTH + GLA_QK) // GLA_QK
    voff = (F_WIDTH + 2 * GLA_QK) // GLA_V
    whole = lambda a: pl.BlockSpec(a.shape, lambda b, c_: (0,) * a.ndim)
    streams = lambda row: [pl.BlockSpec((c, GLA_QK), lambda b, c_: (row(b, c_), qoff)),
                           pl.BlockSpec((c, GLA_QK), lambda b, c_: (row(b, c_), koff)),
                           pl.BlockSpec((c, GLA_V), lambda b, c_: (row(b, c_), voff)),
                           pl.BlockSpec((c, LANES), lambda b, c_: (row(b, c_), 0))]
    return pl.pallas_call(
        functools.partial(_gla_kernel, chunk=c, n_lvl=n_lvl),
        out_shape=(jax.ShapeDtypeStruct((t, GLA_V), BF16), jax.ShapeDtypeStruct((t, GLA_V), BF16)),
        grid=(bsz, nc),
        in_specs=streams(fwd) + streams(bwd) + [whole(w2h), whole(w2l), whole(ba), whole(mt), whole(mk)],
        out_specs=(pl.BlockSpec((c, GLA_V), lambda b, c_: (fwd(b, c_), 0)),
                   pl.BlockSpec((c, GLA_V), lambda b, c_: (bwd(b, c_), 0))),
        scratch_shapes=[pltpu.VMEM((N_DIRS, GLA_HEADS, GLA_DV, GLA_DK), F32)],
        compiler_params=_cparams(("parallel", "arbitrary")),
        name="gla",
    )(p, p, p, a1, p, p, p, a1, w2h, w2l, ba, mt, mk)


def _mix_kernel(yf_ref, of_ref, ob_ref, r_ref, m_ref, x_ref, gn_ref, wf_ref, wg_ref, wo_ref,
                g1_ref, n2g_ref, sc2_ref, sh2_ref, h_ref, n2t_ref):
    o = of_ref[...].astype(F32) + ob_ref[...].astype(F32)
    parts = []
    for h in range(GLA_HEADS):
        oh = o[:, h * GLA_DV:(h + 1) * GLA_DV]
        ms = jnp.mean(oh * oh, axis=-1, keepdims=True)
        parts.append(oh * lax.rsqrt(ms + EPS * GLA_DK) * gn_ref[...])
    og = jnp.concatenate(parts, axis=1) * jax.nn.silu(r_ref[...].astype(F32))
    yg = _dot(og.astype(BF16), wg_ref[...])
    yf = _dot(yf_ref[...].astype(BF16), wf_ref[...])
    d = yf.shape[1]
    m = m_ref[...].astype(F32)
    mixed = jax.nn.sigmoid(m[:, :d]) * yf + jax.nn.sigmoid(m[:, d:]) * yg
    h1 = x_ref[...] + g1_ref[0] * _dot(mixed.astype(BF16), wo_ref[...])
    h_ref[...] = h1
    ms = jnp.mean(h1 * h1, axis=-1, keepdims=True)
    n2 = h1 * lax.rsqrt(ms + EPS) * n2g_ref[...]
    n2t_ref[...] = (n2 * (1.0 + sc2_ref[0]) + sh2_ref[0]).T.astype(BF16)


def _mix(yf, o_f, o_b, p, x2, gn, wf, wg, wo, mod3, n2g, seq, tm=256):
    t, d = x2.shape
    tm = min(tm, seq)
    bidx = lambda i: (i * tm) // seq
    const = lambda shape: pl.BlockSpec(shape, lambda i: (0,) * len(shape))
    roff = (F_WIDTH + 2 * GLA_QK + GLA_V) // GLA_V
    moff = (F_WIDTH + 2 * GLA_QK + 2 * GLA_V) // (2 * d)
    return pl.pallas_call(
        _mix_kernel,
        out_shape=(jax.ShapeDtypeStruct((t, d), F32), jax.ShapeDtypeStruct((d, t), BF16)),
        grid=(t // tm,),
        in_specs=[pl.BlockSpec((tm, F_WIDTH), lambda i: (i, 0)),
                  pl.BlockSpec((tm, GLA_V), lambda i: (i, 0)),
                  pl.BlockSpec((tm, GLA_V), lambda i: (i, 0)),
                  pl.BlockSpec((tm, GLA_V), lambda i: (i, roff)),
                  pl.BlockSpec((tm, 2 * d), lambda i: (i, moff)),
                  pl.BlockSpec((tm, d), lambda i: (i, 0)),
                  const((1, GLA_DV)),
                  const((F_WIDTH, d)), const((GLA_V, d)), const((d, d)),
                  pl.BlockSpec((1, 1, d), lambda i: (bidx(i) * 6 + 2, 0, 0)),
                  const((1, d)),
                  pl.BlockSpec((1, 1, d), lambda i: (bidx(i) * 6 + 4, 0, 0)),
                  pl.BlockSpec((1, 1, d), lambda i: (bidx(i) * 6 + 3, 0, 0))],
        out_specs=(pl.BlockSpec((tm, d), lambda i: (i, 0)), pl.BlockSpec((d, tm), lambda i: (0, i))),
        compiler_params=_cparams(("parallel",)),
        name="mix",
    )(yf, o_f, o_b, p, p, x2, gn.reshape(1, GLA_DV), wf, wg, wo, mod3, n2g.reshape(1, d), mod3, mod3)


def _sorting_network(n):
    def merge(lo, hi, r):
        step = r * 2
        if step < hi - lo:
            yield from merge(lo, hi, step)
            yield from merge(lo + r, hi, step)
            yield from ((i, i + r) for i in range(lo + r, hi - r, step))
        else:
            yield (lo, lo + r)

    def sort(lo, hi):
        if hi - lo >= 1:
            mid = lo + (hi - lo) // 2
            yield from sort(lo, mid)
            yield from sort(mid + 1, hi)
            yield from merge(lo, hi, 1)

    return list(sort(0, n - 1))


def _top_values(s, n):
    sub = s.shape[0] // n
    vs = [s[i * sub:(i + 1) * sub] for i in range(n)]
    for i, j in _sorting_network(n):
        vs[i], vs[j] = jnp.maximum(vs[i], vs[j]), jnp.minimum(vs[i], vs[j])
    rows = []
    for r in range(n):
        mx = jnp.max(vs[0], axis=0, keepdims=True)
        rows.append(mx)
        pop = vs[0] == mx
        for i in range(n - r - 1):
            vs[i] = jnp.where(pop, vs[i + 1], vs[i])
    return jnp.concatenate(rows, axis=0)


def _pair_candidates(e1, e2):
    rows = [e1[0:1] * e2]
    for a in range(1, 8):
        rows.append(e1[a:a + 1] * e2[0:8])
    rows.append(e1[8:16] * e2[0:1])
    return jnp.concatenate(rows, axis=0)


def _peer_sel_kernel(n2t_ref, wqt_ref, keys_ref, e1_ref, e2_ref, th_ref):
    qt = _dot(wqt_ref[...], n2t_ref[...])
    for h in range(PEER_HEADS):
        base = h * 2 * PEER_HALF
        s1 = _dot(keys_ref[h, 0], qt[base:base + PEER_HALF].astype(BF16))
        s2 = _dot(keys_ref[h, 1], qt[base + PEER_HALF:base + 2 * PEER_HALF].astype(BF16))
        v1 = _top_values(s1, PEER_TOPK)
        v2 = _top_values(s2, PEER_TOPK)
        e1 = jnp.exp(v1 - v1[0:1])
        e2 = jnp.exp(v2 - v2[0:1])
        cand = _pair_candidates(e1, e2)
        cur = cand
        rem = jnp.full_like(cand[0:1], float(PEER_TOPK))
        theta = jnp.zeros_like(rem)
        for _ in range(PEER_TOPK):
            mx = jnp.max(cur, axis=0, keepdims=True)
            eq = cur == mx
            theta = jnp.where(rem > 0.0, mx, theta)
            rem = rem - jnp.sum(jnp.where(eq, 1.0, 0.0), axis=0, keepdims=True)
            cur = jnp.where(eq, -1.0, cur)
        sel = cand >= theta
        rz = 1.0 / jnp.sum(jnp.where(sel, cand, 0.0), axis=0, keepdims=True)
        candn = _pair_candidates(e1 * rz, e2)
        th_ref[h:h + 1, :] = jnp.min(jnp.where(sel, candn, jnp.inf), axis=0, keepdims=True)
        e1_ref[h] = jnp.exp(s1 - v1[0:1]) * rz
        e2_ref[h] = jnp.exp(s2 - v2[0:1])


def _peer_sel(n2t, wqt, keys, tm=256):
    d, t = n2t.shape
    tm = min(tm, t)
    nq = wqt.shape[0]
    return pl.pallas_call(
        _peer_sel_kernel,
        out_shape=(jax.ShapeDtypeStruct((PEER_HEADS, N_KEYS, t), F32),
                   jax.ShapeDtypeStruct((PEER_HEADS, N_KEYS, t), F32),
                   jax.ShapeDtypeStruct((PEER_HEADS, t), F32)),
        grid=(t // tm,),
        in_specs=[pl.BlockSpec((d, tm), lambda i: (0, i)),
                  pl.BlockSpec((nq, d), lambda i: (0, 0)),
                  pl.BlockSpec((PEER_HEADS, 2, N_KEYS, PEER_HALF), lambda i: (0, 0, 0, 0))],
        out_specs=(pl.BlockSpec((PEER_HEADS, N_KEYS, tm), lambda i: (0, 0, i)),
                   pl.BlockSpec((PEER_HEADS, N_KEYS, tm), lambda i: (0, 0, i)),
                   pl.BlockSpec((PEER_HEADS, tm), lambda i: (0, i))),
        compiler_params=_cparams(("parallel",)),
        name="peer_sel",
    )(n2t, wqt, keys)


def _peer_kernel(n2t_ref, e1_ref, e2_ref, th_ref, u_ref, vt_ref, out_ref,
                 a_scr, w_scr, *, n_e, ib, tt):
    k = pl.program_id(0)
    e_out = lax.rem(jnp.maximum(k - 2, 0), n_e)

    @pl.when(k == 0)
    def _():
        a_scr[1] = jnp.zeros(a_scr.shape[1:], F32)
        w_scr[0] = jnp.zeros(w_scr.shape[1:], BF16)

    @pl.when((k == 0) | ((k >= 2) & (e_out == 0)))
    def _():
        out_ref[...] = jnp.zeros_like(out_ref)

    def gate_unit(a_prv, w_prv, ii, lc):
        rows = slice(ii * N_KEYS, (ii + 1) * N_KEYS)
        sl = slice(lc * LANES, (lc + 1) * LANES)
        g = jnp.zeros((N_KEYS, LANES), F32)
        for h in range(PEER_HEADS):
            p = e1_ref[h, ii:ii + 1, sl] * e2_ref[h, :, sl]
            g = g + jnp.where(p >= th_ref[h:h + 1, sl], p, 0.0)
        a = a_prv[rows, sl]
        act = 0.5 * a * (1.0 + lax.erf(a * (1.0 / math.sqrt(2.0))))
        w_prv[rows, sl] = (act * g).astype(BF16)

    def stages(a_cur, a_prv, w_cur, w_prv):
        eb, d = a_cur.shape[0], out_ref.shape[0]

        def pre_unit(r, c):
            rows, toks = slice(r * UNIT_ROWS, (r + 1) * UNIT_ROWS), slice(c * MXU_TILE, (c + 1) * MXU_TILE)
            a_cur[rows, toks] = _dot(u_ref[rows, :], n2t_ref[:, toks])

        def val_unit(r, c):
            rows, toks = slice(r * UNIT_ROWS, (r + 1) * UNIT_ROWS), slice(c * MXU_TILE, (c + 1) * MXU_TILE)
            out_ref[rows, toks] += _dot(vt_ref[0, rows, :], w_cur[:, toks])

        n_tc = tt // MXU_TILE
        pre = [functools.partial(pre_unit, r, c) for r in range(eb // UNIT_ROWS) for c in range(n_tc)]
        val = [functools.partial(val_unit, r, c) for r in range(d // UNIT_ROWS) for c in range(n_tc)]
        gate = [functools.partial(gate_unit, a_prv, w_prv, ii, lc) for ii in range(ib) for lc in range(tt // LANES)]
        for unit in _interleave(pre, val, gate):
            unit()

    cur = lax.rem(k, 2)
    prv = 1 - cur
    stages(a_scr.at[cur], a_scr.at[prv], w_scr.at[cur], w_scr.at[prv])


def _peer(n2t, e1, e2, th, u, v, seq, tt=1024, eb=1024):
    d, t = n2t.shape
    tt = min(tt, seq)
    n_e = u.shape[0] // eb
    vt = v.reshape(n_e, eb, d).transpose(0, 2, 1)
    n_blk = (t // tt) * n_e
    ib = eb // N_KEYS
    s1 = lambda k: jnp.minimum(k, n_blk - 1)
    s2 = lambda k: jnp.clip(k - 1, 0, n_blk - 1)
    s3 = lambda k: jnp.clip(k - 2, 0, n_blk - 1)
    once = pl.Buffered(1)
    return pl.pallas_call(
        functools.partial(_peer_kernel, n_e=n_e, ib=ib, tt=tt),
        out_shape=jax.ShapeDtypeStruct((d, t), F32),
        grid=(n_blk + 2,),
        in_specs=[pl.BlockSpec((d, tt), lambda k: (0, s1(k) // n_e), pipeline_mode=once),
                  pl.BlockSpec((PEER_HEADS, ib, tt), lambda k: (0, s2(k) % n_e, s2(k) // n_e)),
                  pl.BlockSpec((PEER_HEADS, N_KEYS, tt), lambda k: (0, 0, s2(k) // n_e), pipeline_mode=once),
                  pl.BlockSpec((PEER_HEADS, tt), lambda k: (0, s2(k) // n_e)),
                  pl.BlockSpec((eb, d), lambda k: (s1(k) % n_e, 0)),
                  pl.BlockSpec((1, d, eb), lambda k: (s3(k) % n_e, 0, 0))],
        out_specs=pl.BlockSpec((d, tt), lambda k: (0, s3(k) // n_e)),
        scratch_shapes=[pltpu.VMEM((2, eb, tt), F32), pltpu.VMEM((2, eb, tt), BF16)],
        compiler_params=_cparams(("arbitrary",)),
        name="peer",
    )(n2t, e1, e2, th, u, vt)


def _peer_out_kernel(pt_ref, h1_ref, g2_ref, fg_ref, out_ref, *, final_norm):
    h2 = h1_ref[...] + g2_ref[0] * pt_ref[...].T
    if final_norm:
        ms = jnp.mean(h2 * h2, axis=-1, keepdims=True)
        h2 = h2 * lax.rsqrt(ms + EPS) * fg_ref[...]
    out_ref[...] = h2


def _peer_out(pt, h1, mod3, fg, seq, final_norm, tm=512):
    t, d = h1.shape
    tm = min(tm, seq)
    bidx = lambda i: (i * tm) // seq
    return pl.pallas_call(
        functools.partial(_peer_out_kernel, final_norm=final_norm),
        out_shape=jax.ShapeDtypeStruct((t, d), F32),
        grid=(t // tm,),
        in_specs=[pl.BlockSpec((d, tm), lambda i: (0, i)),
                  pl.BlockSpec((tm, d), lambda i: (i, 0)),
                  pl.BlockSpec((1, 1, d), lambda i: (bidx(i) * 6 + 5, 0, 0)),
                  pl.BlockSpec((1, d), lambda i: (0, 0))],
        out_specs=pl.BlockSpec((tm, d), lambda i: (i, 0)),
        compiler_params=_cparams(("parallel",)),
        name="peer_out",
    )(pt, h1, mod3, fg.reshape(1, d))


def kernel(x, c, w_ada, b_ada, norm1_g, w_in, w_fnet, gla_w_a2, gla_b_a, gla_norm_g, w_gla, w_out,
           norm2_g, peer_w_q, peer_keys, peer_u, peer_v, final_norm_g):
    bsz, seq, d = x.shape
    t = bsz * seq
    depth = w_ada.shape[0]
    fc, ta, tc = _dft_tables(seq)
    mt, mk = _gla_tables(min(GLA_CHUNK, seq))
    n_main = F_WIDTH + 2 * GLA_QK + 2 * GLA_V
    h = x.reshape(t, d)
    for l in range(depth):
        mod3 = _ada(c, w_ada[l], b_ada[l]).reshape(bsz * 6, 1, d)
        w_main = jnp.concatenate([w_in[l][:, :n_main], w_in[l][:, n_main + N_DIRS * GATE_RANK:]], axis=1)
        w_a1 = jnp.zeros((d, LANES), F32).at[:, :N_DIRS * GATE_RANK].set(
            w_in[l][:, n_main:n_main + N_DIRS * GATE_RANK])
        p, a1 = _inproj(h, norm1_g[l], mod3, w_main.astype(BF16), w_a1.astype(BF16), seq)

        yf = _seq_dft(_chan_dft(p, fc), bsz, seq, ta, tc)

        w2 = jnp.zeros((N_DIRS, LANES, GLA_QK), F32)
        for dr in range(N_DIRS):
            w2 = w2.at[dr, dr * GATE_RANK:(dr + 1) * GATE_RANK].set(gla_w_a2[l][dr])
        w2h = w2.astype(BF16)
        w2l = (w2 - w2h.astype(F32)).astype(BF16)
        ba = gla_b_a[l].reshape(N_DIRS, 1, GLA_QK)
        o_f, o_b = _gla(p, a1, w2h, w2l, ba, mt, mk, bsz, seq)

        h1, n2t = _mix(yf, o_f, o_b, p, h, gla_norm_g[l], w_fnet[l].astype(BF16), w_gla[l].astype(BF16),
                      w_out[l].astype(BF16), mod3, norm2_g[l], seq)

        e1, e2, th = _peer_sel(n2t, peer_w_q[l].T.astype(BF16), peer_keys[l].astype(BF16))
        pt = _peer(n2t, e1, e2, th, peer_u[l].astype(BF16), peer_v[l].astype(BF16), seq)
        h = _peer_out(pt, h1, mod3, final_norm_g, seq, final_norm=(l == depth - 1))
    return h.reshape(bsz, seq, d)
```

```python
import functools
import math

import numpy as np
import jax
import jax.numpy as jnp
from jax import lax
from jax.experimental import pallas as pl
from jax.experimental.pallas import tpu as pltpu

BF16 = jnp.bfloat16
F32 = jnp.float32

EPS = 1e-6
F_GROUPS = 4
F_GROUP_DIM = 256
F_WIDTH = 1024
GLA_HEADS = 4
GLA_DK = 128
GLA_DV = 256
GLA_QK = 512
GLA_V = 1024
GATE_RANK = 16
GATE_TAU = 16.0
N_DIRS = 2
PEER_HEADS = 8
PEER_HALF = 128
N_KEYS = 128
PEER_TOPK = 16

LANES = 128
MXU_TILE = 256
UNIT_ROWS = 256
GLA_CHUNK = 128
FFT_N2 = 128
FFT_GROUP = 8
VMEM_LIMIT = 56 * 1024 * 1024

_NT = (((1,), (1,)), ((), ()))
_TN = (((0,), (0,)), ((), ()))


def _cparams(sem, flags=None):
    return pltpu.CompilerParams(dimension_semantics=sem, vmem_limit_bytes=VMEM_LIMIT, flags=flags)


def _dot(a, b):
    return jnp.dot(a, b, preferred_element_type=F32)


def _interleave(*streams):
    keyed = [((i + 0.5) / len(s), si, i, item) for si, s in enumerate(streams) for i, item in enumerate(s)]
    return [item for _, _, _, item in sorted(keyed, key=lambda t: t[:3])]


def _split_bf16(a):
    hi = a.astype(BF16)
    lo = (a - hi.astype(F32)).astype(BF16)
    return hi, lo


def _ada_kernel(c_ref, w_ref, b_ref, o_ref):
    s = jax.nn.silu(c_ref[...])
    o_ref[...] = _dot(s.astype(BF16), w_ref[...].astype(BF16)) + b_ref[...]


def _ada(c, w, b, tn=1536):
    bsz, d = c.shape
    n = w.shape[1]
    cp = jnp.zeros((8, d), F32).at[:bsz].set(c)
    out = pl.pallas_call(
        _ada_kernel,
        out_shape=jax.ShapeDtypeStruct((8, n), F32),
        grid=(n // tn,),
        in_specs=[pl.BlockSpec((8, d), lambda j: (0, 0)),
                  pl.BlockSpec((d, tn), lambda j: (0, j)),
                  pl.BlockSpec((1, tn), lambda j: (0, j))],
        out_specs=pl.BlockSpec((8, tn), lambda j: (0, j)),
        compiler_params=_cparams(("arbitrary",)),
        name="ada",
    )(cp, w, b.reshape(1, n))
    return out[:bsz]


def _inproj_kernel(x_ref, g_ref, sc_ref, sh_ref, w_ref, wa_ref, p_ref, a1_ref, n_scr):
    @pl.when(pl.program_id(1) == 0)
    def _():
        x = x_ref[...]
        ms = jnp.mean(x * x, axis=-1, keepdims=True)
        y = x * lax.rsqrt(ms + EPS) * g_ref[...]
        nb = (y * (1.0 + sc_ref[0]) + sh_ref[0]).astype(BF16)
        n_scr[...] = nb
        a1_ref[...] = _dot(nb, wa_ref[...])

    p_ref[...] = _dot(n_scr[...], w_ref[0]).astype(BF16)


def _inproj(x2, g, mod3, w, wa, seq, tm=1024, tn=1024):
    t, d = x2.shape
    n = w.shape[1]
    tm = min(tm, seq)
    bidx = lambda i: (i * tm) // seq
    wt = w.reshape(d, n // tn, tn).transpose(1, 0, 2)
    return pl.pallas_call(
        _inproj_kernel,
        out_shape=(jax.ShapeDtypeStruct((t, n), BF16), jax.ShapeDtypeStruct((t, LANES), F32)),
        grid=(t // tm, n // tn),
        in_specs=[pl.BlockSpec((tm, d), lambda i, j: (i, 0)),
                  pl.BlockSpec((1, d), lambda i, j: (0, 0)),
                  pl.BlockSpec((1, 1, d), lambda i, j: (bidx(i) * 6 + 1, 0, 0)),
                  pl.BlockSpec((1, 1, d), lambda i, j: (bidx(i) * 6, 0, 0)),
                  pl.BlockSpec((1, d, tn), lambda i, j: (j, 0, 0)),
                  pl.BlockSpec((d, LANES), lambda i, j: (0, 0))],
        out_specs=(pl.BlockSpec((tm, tn), lambda i, j: (i, j)),
                   pl.BlockSpec((tm, LANES), lambda i, j: (i, 0))),
        scratch_shapes=[pltpu.VMEM((tm, d), BF16)],
        compiler_params=_cparams(("parallel", "arbitrary")),
        name="inproj",
    )(x2, g.reshape(1, d), mod3, mod3, wt, wa)


def _chan_dft_kernel(z_ref, f_ref, w_ref):
    for g in range(F_GROUPS):
        lo, hi = g * F_GROUP_DIM, (g + 1) * F_GROUP_DIM
        r = _dot(z_ref[:, lo:hi], f_ref[...])
        w_ref[:, lo:hi] = r[:, :F_GROUP_DIM]
        w_ref[:, F_WIDTH + lo:F_WIDTH + hi] = r[:, F_GROUP_DIM:]


def _chan_dft(p, fc, tm=1024):
    t = p.shape[0]
    tm = min(tm, t)
    return pl.pallas_call(
        _chan_dft_kernel,
        out_shape=jax.ShapeDtypeStruct((t, 2 * F_WIDTH), F32),
        grid=(t // tm,),
        in_specs=[pl.BlockSpec((tm, F_WIDTH), lambda i: (i, 0)),
                  pl.BlockSpec((F_GROUP_DIM, 2 * F_GROUP_DIM), lambda i: (0, 0))],
        out_specs=pl.BlockSpec((tm, 2 * F_WIDTH), lambda i: (i, 0)),
        compiler_params=_cparams(("parallel",)),
        name="chan_dft",
    )(p, fc)


def _fft_a_kernel(w_ref, t_ref, x_ref):
    for j in range(FFT_GROUP):
        w = w_ref[0, :, j, :]
        rot = jnp.concatenate([w[:, F_WIDTH:], -w[:, :F_WIDTH]], axis=1)
        rhs = jnp.concatenate([w, rot], axis=0).astype(BF16)
        x_ref[0, j] = _dot(t_ref[j], rhs)


def _fft_c_kernel(x_ref, t_ref, y_ref):
    for j in range(FFT_GROUP):
        x = x_ref[0, :, j, :]
        rhs = jnp.concatenate([x[:, :F_WIDTH], x[:, F_WIDTH:]], axis=0).astype(BF16)
        y_ref[0, :, j, :] = _dot(t_ref[...], rhs)


def _seq_dft(wc, bsz, seq, ta, tc):
    n2 = FFT_N2
    n1 = seq // n2
    g = FFT_GROUP
    x1 = pl.pallas_call(
        _fft_a_kernel,
        out_shape=jax.ShapeDtypeStruct((bsz, n2, n1, 2 * F_WIDTH), F32),
        grid=(bsz, n2 // g),
        in_specs=[pl.BlockSpec((1, n1, g, 2 * F_WIDTH), lambda b, s: (b, 0, s, 0)),
                  pl.BlockSpec((g, n1, 2 * n1), lambda b, s: (s, 0, 0))],
        out_specs=pl.BlockSpec((1, g, n1, 2 * F_WIDTH), lambda b, s: (b, s, 0, 0)),
        compiler_params=_cparams(("parallel", "parallel")),
        name="fft_a",
    )(wc.reshape(bsz, n1, n2, 2 * F_WIDTH), ta)
    y = pl.pallas_call(
        _fft_c_kernel,
        out_shape=jax.ShapeDtypeStruct((bsz, n2, n1, F_WIDTH), F32),
        grid=(bsz, n1 // g),
        in_specs=[pl.BlockSpec((1, n2, g, 2 * F_WIDTH), lambda b, k: (b, 0, k, 0)),
                  pl.BlockSpec((n2, 2 * n2), lambda b, k: (0, 0))],
        out_specs=pl.BlockSpec((1, n2, g, F_WIDTH), lambda b, k: (b, 0, k, 0)),
        compiler_params=_cparams(("parallel", "parallel")),
        name="fft_c",
    )(x1, tc)
    return y.reshape(bsz * seq, F_WIDTH)


def _dft_tables(seq):
    n2 = FFT_N2
    n1 = seq // n2
    kc = np.arange(F_GROUP_DIM)
    ang = 2.0 * np.pi * ((kc[:, None] * kc[None, :]) % F_GROUP_DIM) / F_GROUP_DIM
    sc = 1.0 / math.sqrt(F_GROUP_DIM)
    fc = np.concatenate([np.cos(ang) * sc, -np.sin(ang) * sc], axis=1)
    k1 = np.arange(n1)[None, :, None]
    s1 = np.arange(n1)[None, None, :]
    s2 = np.arange(n2)[:, None, None]
    ang = 2.0 * np.pi * ((k1 * (s2 + n2 * s1)) % seq) / seq
    sa = 1.0 / math.sqrt(seq)
    ta = np.concatenate([np.cos(ang) * sa, np.sin(ang) * sa], axis=2)
    k2 = np.arange(n2)
    ang = 2.0 * np.pi * ((k2[:, None] * k2[None, :]) % n2) / n2
    tc = np.concatenate([np.cos(ang), np.sin(ang)], axis=1)
    as_bf16 = lambda a: jnp.asarray(a.astype(np.float32)).astype(BF16)
    return as_bf16(fc), as_bf16(ta), as_bf16(tc)


def _gla_levels(c):
    halves = []
    s = c // 2
    while s >= 1:
        halves.append(s)
        s //= 2
    return halves


def _gla_tables(c):
    r = np.arange(c)[:, None]
    l = np.arange(c)[None, :]
    mats = [(l <= r), (l > r)]
    masks = []
    for s in _gla_levels(c):
        m = (r // (2 * s)) * (2 * s) + s - 1
        q_side = r > m
        mats.append(np.where(q_side, (l > m) & (l <= r), (l > r) & (l <= m)))
        i, j = r, l
        masks.append((i // (2 * s) == j // (2 * s)) & (i % (2 * s) >= s) & (j % (2 * s) < s))
    masks.append(r == l)
    mt = np.stack(mats).astype(np.float32)
    mk = np.stack(masks).astype(np.float32)
    mt = np.stack([mt, mt[:, ::-1, ::-1]])
    mk = np.stack([mk, mk[:, ::-1, ::-1]])
    nt = mt.shape[1]
    return (jnp.asarray(mt.reshape(2, nt * c, c)).astype(BF16), jnp.asarray(mk))


def _gla_kernel(qf_ref, kf_ref, vf_ref, af_ref, qb_ref, kb_ref, vb_ref, ab_ref,
                w2h_ref, w2l_ref, ba_ref, mt_ref, mk_ref, of_ref, ob_ref, st_ref, *, chunk, n_lvl):
    c = chunk
    dirs = range(N_DIRS)
    heads = range(GLA_HEADS)
    dk = lambda h: slice(h * GLA_DK, (h + 1) * GLA_DK)
    dv = lambda h: slice(h * GLA_DV, (h + 1) * GLA_DV)
    q_refs, k_refs, v_refs = (qf_ref, qb_ref), (kf_ref, kb_ref), (vf_ref, vb_ref)
    a_refs, o_refs = (af_ref, ab_ref), (of_ref, ob_ref)

    @pl.when(pl.program_id(1) == 0)
    def _():
        st_ref[...] = jnp.zeros_like(st_ref)

    ex = []
    for d in dirs:
        a1h, a1l = _split_bf16(a_refs[d][...])
        w2h = w2h_ref[d]
        z = _dot(a1h, w2h) + _dot(a1l, w2h) + _dot(a1h, w2l_ref[d]) + ba_ref[d]
        la = jax.nn.log_sigmoid(z) * (1.0 / GATE_TAU)
        lah, lal = _split_bf16(la)
        e = _dot(mt_ref[d], jnp.concatenate([lah, lal], axis=1))
        ex.append(jnp.exp(e[:, :GLA_QK] + e[:, GLA_QK:]))

    qb = [q_refs[d][...] for d in dirs]
    kb = [k_refs[d][...] for d in dirs]
    q = [x.astype(F32) for x in qb]
    k = [x.astype(F32) for x in kb]
    attn = [[lax.dot_general(qb[d][:, dk(h)], kb[d][:, dk(h)], _NT, preferred_element_type=F32)
             * mk_ref[d, n_lvl] for h in heads] for d in dirs]
    for lvl in range(n_lvl):
        for d in dirs:
            e = ex[d][(2 + lvl) * c:(3 + lvl) * c]
            qe = (q[d] * e).astype(BF16)
            ke = (k[d] * e).astype(BF16)
            for h in heads:
                a = lax.dot_general(qe[:, dk(h)], ke[:, dk(h)], _NT, preferred_element_type=F32)
                attn[d][h] = attn[d][h] + a * mk_ref[d, lvl]
    for d in dirs:
        e_cum = ex[d][0:c]
        e_rev = ex[d][c:2 * c]
        qc = (q[d] * e_cum).astype(BF16)
        kd = (k[d] * e_rev).astype(BF16)
        tot = e_cum[0:1] * e_rev[0:1]
        for h in heads:
            v = v_refs[d][:, dv(h)]
            st = st_ref[d, h]
            o = _dot(attn[d][h].astype(BF16), v)
            o = o + lax.dot_general(qc[:, dk(h)], st.astype(BF16), _NT, preferred_element_type=F32)
            o_refs[d][:, dv(h)] = o.astype(BF16)
            st_ref[d, h] = st * tot[:, dk(h)] + lax.dot_general(v, kd[:, dk(h)], _TN,
                                                                preferred_element_type=F32)


def _gla(p, a1, w2h, w2l, ba, mt, mk, bsz, seq):
    t = p.shape[0]
    c = min(GLA_CHUNK, seq)
    nc = seq // c
    n_lvl = len(_gla_levels(c))
    fwd = lambda b, c_: b * nc + c_
    bwd = lambda b, c_: b * nc + nc - 1 - c_
    qoff = F_WIDTH // GLA_QK
    koff = (F_WIDTH + GLA_QK) // GLA_QK
    voff = (F_WIDTH + 2 * GLA_QK) // GLA_V
    whole = lambda a: pl.BlockSpec(a.shape, lambda b, c_: (0,) * a.ndim)
    streams = lambda row: [pl.BlockSpec((c, GLA_QK), lambda b, c_: (row(b, c_), qoff)),
                           pl.BlockSpec((c, GLA_QK), lambda b, c_: (row(b, c_), koff)),
                           pl.BlockSpec((c, GLA_V), lambda b, c_: (row(b, c_), voff)),
                           pl.BlockSpec((c, LANES), lambda b, c_: (row(b, c_), 0))]
    return pl.pallas_call(
        functools.partial(_gla_kernel, chunk=c, n_lvl=n_lvl),
        out_shape=(jax.ShapeDtypeStruct((t, GLA_V), BF16), jax.ShapeDtypeStruct((t, GLA_V), BF16)),
        grid=(bsz, nc),
        in_specs=streams(fwd) + streams(bwd) + [whole(w2h), whole(w2l), whole(ba), whole(mt), whole(mk)],
        out_specs=(pl.BlockSpec((c, GLA_V), lambda b, c_: (fwd(b, c_), 0)),
                   pl.BlockSpec((c, GLA_V), lambda b, c_: (bwd(b, c_), 0))),
        scratch_shapes=[pltpu.VMEM((N_DIRS, GLA_HEADS, GLA_DV, GLA_DK), F32)],
        compiler_params=_cparams(("parallel", "arbitrary")),
        name="gla",
    )(p, p, p, a1, p, p, p, a1, w2h, w2l, ba, mt, mk)


def _mix_kernel(yf_ref, of_ref, ob_ref, r_ref, m_ref, x_ref, gn_ref, wf_ref, wg_ref, wo_ref,
                g1_ref, n2g_ref, sc2_ref, sh2_ref, h_ref, n2t_ref):
    o = of_ref[...].astype(F32) + ob_ref[...].astype(F32)
    parts = []
    for h in range(GLA_HEADS):
        oh = o[:, h * GLA_DV:(h + 1) * GLA_DV]
        ms = jnp.mean(oh * oh, axis=-1, keepdims=True)
        parts.append(oh * lax.rsqrt(ms + EPS * GLA_DK) * gn_ref[...])
    og = jnp.concatenate(parts, axis=1) * jax.nn.silu(r_ref[...].astype(F32))
    yg = _dot(og.astype(BF16), wg_ref[...])
    yf = _dot(yf_ref[...].astype(BF16), wf_ref[...])
    d = yf.shape[1]
    m = m_ref[...].astype(F32)
    mixed = jax.nn.sigmoid(m[:, :d]) * yf + jax.nn.sigmoid(m[:, d:]) * yg
    h1 = x_ref[...] + g1_ref[0] * _dot(mixed.astype(BF16), wo_ref[...])
    h_ref[...] = h1
    ms = jnp.mean(h1 * h1, axis=-1, keepdims=True)
    n2 = h1 * lax.rsqrt(ms + EPS) * n2g_ref[...]
    n2t_ref[...] = (n2 * (1.0 + sc2_ref[0]) + sh2_ref[0]).T.astype(BF16)


def _mix(yf, o_f, o_b, p, x2, gn, wf, wg, wo, mod3, n2g, seq, tm=256):
    t, d = x2.shape
    tm = min(tm, seq)
    bidx = lambda i: (i * tm) // seq
    const = lambda shape: pl.BlockSpec(shape, lambda i: (0,) * len(shape))
    roff = (F_WIDTH + 2 * GLA_QK + GLA_V) // GLA_V
    moff = (F_WIDTH + 2 * GLA_QK + 2 * GLA_V) // (2 * d)
    return pl.pallas_call(
        _mix_kernel,
        out_shape=(jax.ShapeDtypeStruct((t, d), F32), jax.ShapeDtypeStruct((d, t), BF16)),
        grid=(t // tm,),
        in_specs=[pl.BlockSpec((tm, F_WIDTH), lambda i: (i, 0)),
                  pl.BlockSpec((tm, GLA_V), lambda i: (i, 0)),
                  pl.BlockSpec((tm, GLA_V), lambda i: (i, 0)),
                  pl.BlockSpec((tm, GLA_V), lambda i: (i, roff)),
                  pl.BlockSpec((tm, 2 * d), lambda i: (i, moff)),
                  pl.BlockSpec((tm, d), lambda i: (i, 0)),
                  const((1, GLA_DV)),
                  const((F_WIDTH, d)), const((GLA_V, d)), const((d, d)),
                  pl.BlockSpec((1, 1, d), lambda i: (bidx(i) * 6 + 2, 0, 0)),
                  const((1, d)),
                  pl.BlockSpec((1, 1, d), lambda i: (bidx(i) * 6 + 4, 0, 0)),
                  pl.BlockSpec((1, 1, d), lambda i: (bidx(i) * 6 + 3, 0, 0))],
        out_specs=(pl.BlockSpec((tm, d), lambda i: (i, 0)), pl.BlockSpec((d, tm), lambda i: (0, i))),
        compiler_params=_cparams(("parallel",)),
        name="mix",
    )(yf, o_f, o_b, p, p, x2, gn.reshape(1, GLA_DV), wf, wg, wo, mod3, n2g.reshape(1, d), mod3, mod3)


def _sorting_network(n):
    def merge(lo, hi, r):
        step = r * 2
        if step < hi - lo:
            yield from merge(lo, hi, step)
            yield from merge(lo + r, hi, step)
            yield from ((i, i + r) for i in range(lo + r, hi - r, step))
        else:
            yield (lo, lo + r)

    def sort(lo, hi):
        if hi - lo >= 1:
            mid = lo + (hi - lo) // 2
            yield from sort(lo, mid)
            yield from sort(mid + 1, hi)
            yield from merge(lo, hi, 1)

    return list(sort(0, n - 1))


def _top_values(s, n):
    sub = s.shape[0] // n
    vs = [s[i * sub:(i + 1) * sub] for i in range(n)]
    for i, j in _sorting_network(n):
        vs[i], vs[j] = jnp.maximum(vs[i], vs[j]), jnp.minimum(vs[i], vs[j])
    rows = []
    for r in range(n):
        mx = jnp.max(vs[0], axis=0, keepdims=True)
        rows.append(mx)
        pop = vs[0] == mx
        for i in range(n - r - 1):
            vs[i] = jnp.where(pop, vs[i + 1], vs[i])
    return jnp.concatenate(rows, axis=0)


def _pair_candidates(e1, e2):
    rows = [e1[0:1] * e2]
    for a in range(1, 8):
        rows.append(e1[a:a + 1] * e2[0:8])
    rows.append(e1[8:16] * e2[0:1])
    return jnp.concatenate(rows, axis=0)


def _kth_largest_product(e1, e2, k):
    half = k // 2
    head = e1 * e2[0:1]
    deep = [e1[:half] * e2[b:b + 1] for b in range(1, k)]
    rem = jnp.full_like(head[0:1], float(k))
    theta = jnp.zeros_like(rem)
    for r in range(k):
        mx = jnp.max(head, axis=0, keepdims=True)
        pop = head == mx
        theta = jnp.where(rem > 0.0, mx, theta)
        rem = rem - jnp.sum(jnp.where(pop, 1.0, 0.0), axis=0, keepdims=True)
        if r + 1 < k:
            head = jnp.concatenate([jnp.where(pop[:half], deep[0], head[:half]),
                                    jnp.where(pop[half:], -1.0, head[half:])], axis=0)
            for i in range(len(deep) - 1 - r):
                deep[i] = jnp.where(pop[:half], deep[i + 1], deep[i])
    return theta


def _peer_sel_kernel(n2t_ref, wqt_ref, keys_ref, e1_ref, e2_ref, th_ref):
    qt = _dot(wqt_ref[...], n2t_ref[...])
    for h in range(PEER_HEADS):
        base = h * 2 * PEER_HALF
        s1 = _dot(keys_ref[h, 0], qt[base:base + PEER_HALF].astype(BF16))
        s2 = _dot(keys_ref[h, 1], qt[base + PEER_HALF:base + 2 * PEER_HALF].astype(BF16))
        v1 = _top_values(s1, PEER_TOPK)
        v2 = _top_values(s2, PEER_TOPK)
        e1 = jnp.exp(v1 - v1[0:1])
        e2 = jnp.exp(v2 - v2[0:1])
        cand = _pair_candidates(e1, e2)
        theta = _kth_largest_product(e1, e2, PEER_TOPK)
        sel = cand >= theta
        rz = 1.0 / jnp.sum(jnp.where(sel, cand, 0.0), axis=0, keepdims=True)
        candn = _pair_candidates(e1 * rz, e2)
        th_ref[h:h + 1, :] = jnp.min(jnp.where(sel, candn, jnp.inf), axis=0, keepdims=True)
        e1_ref[h] = jnp.exp(s1 - v1[0:1]) * rz
        e2_ref[h] = jnp.exp(s2 - v2[0:1])


def _peer_sel(n2t, wqt, keys, tm=256):
    d, t = n2t.shape
    tm = min(tm, t)
    nq = wqt.shape[0]
    return pl.pallas_call(
        _peer_sel_kernel,
        out_shape=(jax.ShapeDtypeStruct((PEER_HEADS, N_KEYS, t), F32),
                   jax.ShapeDtypeStruct((PEER_HEADS, N_KEYS, t), F32),
                   jax.ShapeDtypeStruct((PEER_HEADS, t), F32)),
        grid=(t // tm,),
        in_specs=[pl.BlockSpec((d, tm), lambda i: (0, i)),
                  pl.BlockSpec((nq, d), lambda i: (0, 0)),
                  pl.BlockSpec((PEER_HEADS, 2, N_KEYS, PEER_HALF), lambda i: (0, 0, 0, 0))],
        out_specs=(pl.BlockSpec((PEER_HEADS, N_KEYS, tm), lambda i: (0, 0, i)),
                   pl.BlockSpec((PEER_HEADS, N_KEYS, tm), lambda i: (0, 0, i)),
                   pl.BlockSpec((PEER_HEADS, tm), lambda i: (0, i))),
        compiler_params=_cparams(("parallel",)),
        name="peer_sel",
    )(n2t, wqt, keys)


def _peer_kernel(n2t_ref, e1_ref, e2_ref, th_ref, u_ref, vt_ref, out_ref,
                 a_scr, w_scr, *, n_e, ib, tt):
    k = pl.program_id(0)
    e_out = lax.rem(jnp.maximum(k - 2, 0), n_e)

    @pl.when(k == 0)
    def _():
        a_scr[1] = jnp.zeros(a_scr.shape[1:], F32)
        w_scr[0] = jnp.zeros(w_scr.shape[1:], BF16)

    @pl.when((k == 0) | ((k >= 2) & (e_out == 0)))
    def _():
        out_ref[...] = jnp.zeros_like(out_ref)

    def gate_unit(a_prv, w_prv, ii, lc):
        rows = slice(ii * N_KEYS, (ii + 1) * N_KEYS)
        sl = slice(lc * LANES, (lc + 1) * LANES)
        g = jnp.zeros((N_KEYS, LANES), F32)
        for h in range(PEER_HEADS):
            p = e1_ref[h, ii:ii + 1, sl] * e2_ref[h, :, sl]
            g = g + jnp.where(p >= th_ref[h:h + 1, sl], p, 0.0)
        a = a_prv[rows, sl]
        act = 0.5 * a * (1.0 + lax.erf(a * (1.0 / math.sqrt(2.0))))
        w_prv[rows, sl] = (act * g).astype(BF16)

    def stages(a_cur, a_prv, w_cur, w_prv):
        eb, d = a_cur.shape[0], out_ref.shape[0]

        def pre_unit(r, c):
            rows, toks = slice(r * UNIT_ROWS, (r + 1) * UNIT_ROWS), slice(c * MXU_TILE, (c + 1) * MXU_TILE)
            a_cur[rows, toks] = _dot(u_ref[rows, :], n2t_ref[:, toks])

        def val_unit(r, c):
            rows, toks = slice(r * UNIT_ROWS, (r + 1) * UNIT_ROWS), slice(c * MXU_TILE, (c + 1) * MXU_TILE)
            out_ref[rows, toks] += _dot(vt_ref[0, rows, :], w_cur[:, toks])

        n_tc = tt // MXU_TILE
        pre = [functools.partial(pre_unit, r, c) for r in range(eb // UNIT_ROWS) for c in range(n_tc)]
        val = [functools.partial(val_unit, r, c) for r in range(d // UNIT_ROWS) for c in range(n_tc)]
        gate = [functools.partial(gate_unit, a_prv, w_prv, ii, lc) for ii in range(ib) for lc in range(tt // LANES)]
        for unit in _interleave(pre, val, gate):
            unit()

    cur = lax.rem(k, 2)
    prv = 1 - cur
    stages(a_scr.at[cur], a_scr.at[prv], w_scr.at[cur], w_scr.at[prv])


def _peer(n2t, e1, e2, th, u, v, seq, tt=1024, eb=1024):
    d, t = n2t.shape
    tt = min(tt, seq)
    n_e = u.shape[0] // eb
    vt = v.reshape(n_e, eb, d).transpose(0, 2, 1)
    n_blk = (t // tt) * n_e
    ib = eb // N_KEYS
    s1 = lambda k: jnp.minimum(k, n_blk - 1)
    s2 = lambda k: jnp.clip(k - 1, 0, n_blk - 1)
    s3 = lambda k: jnp.clip(k - 2, 0, n_blk - 1)
    once = pl.Buffered(1)
    return pl.pallas_call(
        functools.partial(_peer_kernel, n_e=n_e, ib=ib, tt=tt),
        out_shape=jax.ShapeDtypeStruct((d, t), F32),
        grid=(n_blk + 2,),
        in_specs=[pl.BlockSpec((d, tt), lambda k: (0, s1(k) // n_e), pipeline_mode=once),
                  pl.BlockSpec((PEER_HEADS, ib, tt), lambda k: (0, s2(k) % n_e, s2(k) // n_e)),
                  pl.BlockSpec((PEER_HEADS, N_KEYS, tt), lambda k: (0, 0, s2(k) // n_e), pipeline_mode=once),
                  pl.BlockSpec((PEER_HEADS, tt), lambda k: (0, s2(k) // n_e)),
                  pl.BlockSpec((eb, d), lambda k: (s1(k) % n_e, 0)),
                  pl.BlockSpec((1, d, eb), lambda k: (s3(k) % n_e, 0, 0))],
        out_specs=pl.BlockSpec((d, tt), lambda k: (0, s3(k) // n_e)),
        scratch_shapes=[pltpu.VMEM((2, eb, tt), F32), pltpu.VMEM((2, eb, tt), BF16)],
        compiler_params=_cparams(("arbitrary",)),
        name="peer",
    )(n2t, e1, e2, th, u, vt)


def _peer_out_kernel(pt_ref, h1_ref, g2_ref, fg_ref, out_ref, *, final_norm):
    h2 = h1_ref[...] + g2_ref[0] * pt_ref[...].T
    if final_norm:
        ms = jnp.mean(h2 * h2, axis=-1, keepdims=True)
        h2 = h2 * lax.rsqrt(ms + EPS) * fg_ref[...]
    out_ref[...] = h2


def _peer_out(pt, h1, mod3, fg, seq, final_norm, tm=512):
    t, d = h1.shape
    tm = min(tm, seq)
    bidx = lambda i: (i * tm) // seq
    return pl.pallas_call(
        functools.partial(_peer_out_kernel, final_norm=final_norm),
        out_shape=jax.ShapeDtypeStruct((t, d), F32),
        grid=(t // tm,),
        in_specs=[pl.BlockSpec((d, tm), lambda i: (0, i)),
                  pl.BlockSpec((tm, d), lambda i: (i, 0)),
                  pl.BlockSpec((1, 1, d), lambda i: (bidx(i) * 6 + 5, 0, 0)),
                  pl.BlockSpec((1, d), lambda i: (0, 0))],
        out_specs=pl.BlockSpec((tm, d), lambda i: (i, 0)),
        compiler_params=_cparams(("parallel",)),
        name="peer_out",
    )(pt, h1, mod3, fg.reshape(1, d))


def kernel(x, c, w_ada, b_ada, norm1_g, w_in, w_fnet, gla_w_a2, gla_b_a, gla_norm_g, w_gla, w_out,
           norm2_g, peer_w_q, peer_keys, peer_u, peer_v, final_norm_g):
    bsz, seq, d = x.shape
    t = bsz * seq
    depth = w_ada.shape[0]
    fc, ta, tc = _dft_tables(seq)
    mt, mk = _gla_tables(min(GLA_CHUNK, seq))
    n_main = F_WIDTH + 2 * GLA_QK + 2 * GLA_V
    h = x.reshape(t, d)
    for l in range(depth):
        mod3 = _ada(c, w_ada[l], b_ada[l]).reshape(bsz * 6, 1, d)
        w_main = jnp.concatenate([w_in[l][:, :n_main], w_in[l][:, n_main + N_DIRS * GATE_RANK:]], axis=1)
        w_a1 = jnp.zeros((d, LANES), F32).at[:, :N_DIRS * GATE_RANK].set(
            w_in[l][:, n_main:n_main + N_DIRS * GATE_RANK])
        p, a1 = _inproj(h, norm1_g[l], mod3, w_main.astype(BF16), w_a1.astype(BF16), seq)

        yf = _seq_dft(_chan_dft(p, fc), bsz, seq, ta, tc)

        w2 = jnp.zeros((N_DIRS, LANES, GLA_QK), F32)
        for dr in range(N_DIRS):
            w2 = w2.at[dr, dr * GATE_RANK:(dr + 1) * GATE_RANK].set(gla_w_a2[l][dr])
        w2h = w2.astype(BF16)
        w2l = (w2 - w2h.astype(F32)).astype(BF16)
        ba = gla_b_a[l].reshape(N_DIRS, 1, GLA_QK)
        o_f, o_b = _gla(p, a1, w2h, w2l, ba, mt, mk, bsz, seq)

        h1, n2t = _mix(yf, o_f, o_b, p, h, gla_norm_g[l], w_fnet[l].astype(BF16), w_gla[l].astype(BF16),
                      w_out[l].astype(BF16), mod3, norm2_g[l], seq)

        e1, e2, th = _peer_sel(n2t, peer_w_q[l].T.astype(BF16), peer_keys[l].astype(BF16))
        pt = _peer(n2t, e1, e2, th, peer_u[l].astype(BF16), peer_v[l].astype(BF16), seq)
        h = _peer_out(pt, h1, mod3, final_norm_g, seq, final_norm=(l == depth - 1))
    return h.reshape(bsz, seq, d)
```

```python
import functools
import math

import numpy as np
import jax
import jax.numpy as jnp
from jax import lax
from jax.experimental import pallas as pl
from jax.experimental.pallas import tpu as pltpu

BF16 = jnp.bfloat16
F32 = jnp.float32

EPS = 1e-6
F_GROUPS = 4
F_GROUP_DIM = 256
F_WIDTH = 1024
GLA_HEADS = 4
GLA_DK = 128
GLA_DV = 256
GLA_QK = 512
GLA_V = 1024
GATE_RANK = 16
GATE_TAU = 16.0
N_DIRS = 2
PEER_HEADS = 8
PEER_HALF = 128
N_KEYS = 128
PEER_TOPK = 16
MOD_SHIFT1, MOD_SCALE1, MOD_GATE1, MOD_SHIFT2, MOD_SCALE2, MOD_GATE2 = range(6)
N_MOD = 6

LANES = 128
SUBLANES = 8
MXU_TILE = 256
UNIT_ROWS = 1024
GLA_CHUNK = 128
FFT_N2 = 128
FFT_GROUP = 8
VMEM_LIMIT = 56 * 1024 * 1024

_NT = (((1,), (1,)), ((), ()))
_TN = (((0,), (0,)), ((), ()))


def _cparams(sem):
    return pltpu.CompilerParams(dimension_semantics=sem, vmem_limit_bytes=VMEM_LIMIT)


def _dot(a, b):
    return jnp.dot(a, b, preferred_element_type=F32)


def _interleave(*streams):
    keyed = [((i + 0.5) / len(s), si, i, item) for si, s in enumerate(streams) for i, item in enumerate(s)]
    return [item for _, _, _, item in sorted(keyed, key=lambda t: t[:3])]


def _split_bf16(a):
    hi = a.astype(BF16)
    lo = (a - hi.astype(F32)).astype(BF16)
    return hi, lo


def _ada_kernel(c_ref, w_ref, b_ref, o_ref):
    s = jax.nn.silu(c_ref[...])
    o_ref[...] = _dot(s.astype(BF16), w_ref[...].astype(BF16)) + b_ref[...]


def _ada(c, w, b, tn=1536):
    bsz, d = c.shape
    n = w.shape[1]
    cp = jnp.zeros((SUBLANES, d), F32).at[:bsz].set(c)
    out = pl.pallas_call(
        _ada_kernel,
        out_shape=jax.ShapeDtypeStruct((SUBLANES, n), F32),
        grid=(n // tn,),
        in_specs=[pl.BlockSpec((SUBLANES, d), lambda j: (0, 0)),
                  pl.BlockSpec((d, tn), lambda j: (0, j)),
                  pl.BlockSpec((1, tn), lambda j: (0, j))],
        out_specs=pl.BlockSpec((SUBLANES, tn), lambda j: (0, j)),
        compiler_params=_cparams(("arbitrary",)),
        name="ada",
    )(cp, w, b.reshape(1, n))
    return out[:bsz]


def _inproj_kernel(x_ref, g_ref, sc_ref, sh_ref, w_ref, wa_ref, p_ref, a1_ref, n_scr):
    @pl.when(pl.program_id(1) == 0)
    def _():
        x = x_ref[...]
        ms = jnp.mean(x * x, axis=-1, keepdims=True)
        y = x * lax.rsqrt(ms + EPS) * g_ref[...]
        nb = (y * (1.0 + sc_ref[0]) + sh_ref[0]).astype(BF16)
        n_scr[...] = nb
        a1_ref[...] = _dot(nb, wa_ref[...])

    p_ref[...] = _dot(n_scr[...], w_ref[0]).astype(BF16)


def _inproj(x2, g, mod3, w, wa, seq, tm=1024, tn=1024):
    t, d = x2.shape
    n = w.shape[1]
    tm = min(tm, seq)
    bidx = lambda i: (i * tm) // seq
    wt = w.reshape(d, n // tn, tn).transpose(1, 0, 2)
    return pl.pallas_call(
        _inproj_kernel,
        out_shape=(jax.ShapeDtypeStruct((t, n), BF16), jax.ShapeDtypeStruct((t, LANES), F32)),
        grid=(t // tm, n // tn),
        in_specs=[pl.BlockSpec((tm, d), lambda i, j: (i, 0)),
                  pl.BlockSpec((1, d), lambda i, j: (0, 0)),
                  pl.BlockSpec((1, 1, d), lambda i, j: (bidx(i) * N_MOD + MOD_SCALE1, 0, 0)),
                  pl.BlockSpec((1, 1, d), lambda i, j: (bidx(i) * N_MOD + MOD_SHIFT1, 0, 0)),
                  pl.BlockSpec((1, d, tn), lambda i, j: (j, 0, 0)),
                  pl.BlockSpec((d, LANES), lambda i, j: (0, 0))],
        out_specs=(pl.BlockSpec((tm, tn), lambda i, j: (i, j)),
                   pl.BlockSpec((tm, LANES), lambda i, j: (i, 0))),
        scratch_shapes=[pltpu.VMEM((tm, d), BF16)],
        compiler_params=_cparams(("parallel", "arbitrary")),
        name="inproj",
    )(x2, g.reshape(1, d), mod3, mod3, wt, wa)


def _chan_dft_kernel(z_ref, f_ref, w_ref):
    for g in range(F_GROUPS):
        lo, hi = g * F_GROUP_DIM, (g + 1) * F_GROUP_DIM
        r = _dot(z_ref[:, lo:hi], f_ref[...])
        w_ref[:, lo:hi] = r[:, :F_GROUP_DIM]
        w_ref[:, F_WIDTH + lo:F_WIDTH + hi] = r[:, F_GROUP_DIM:]


def _chan_dft(p, fc, tm=1024):
    t = p.shape[0]
    tm = min(tm, t)
    return pl.pallas_call(
        _chan_dft_kernel,
        out_shape=jax.ShapeDtypeStruct((t, 2 * F_WIDTH), F32),
        grid=(t // tm,),
        in_specs=[pl.BlockSpec((tm, F_WIDTH), lambda i: (i, 0)),
                  pl.BlockSpec((F_GROUP_DIM, 2 * F_GROUP_DIM), lambda i: (0, 0))],
        out_specs=pl.BlockSpec((tm, 2 * F_WIDTH), lambda i: (i, 0)),
        compiler_params=_cparams(("parallel",)),
        name="chan_dft",
    )(p, fc)


def _fft_a_kernel(w_ref, t_ref, x_ref):
    for j in range(FFT_GROUP):
        w = w_ref[0, :, j, :]
        rot = jnp.concatenate([w[:, F_WIDTH:], -w[:, :F_WIDTH]], axis=1)
        rhs = jnp.concatenate([w, rot], axis=0).astype(BF16)
        x_ref[0, j] = _dot(t_ref[j], rhs)


def _fft_c_kernel(x_ref, t_ref, y_ref):
    for j in range(FFT_GROUP):
        x = x_ref[0, :, j, :]
        rhs = jnp.concatenate([x[:, :F_WIDTH], x[:, F_WIDTH:]], axis=0).astype(BF16)
        y_ref[0, :, j, :] = _dot(t_ref[...], rhs)


def _seq_dft(wc, bsz, seq, ta, tc):
    n2 = FFT_N2
    n1 = seq // n2
    g = FFT_GROUP
    x1 = pl.pallas_call(
        _fft_a_kernel,
        out_shape=jax.ShapeDtypeStruct((bsz, n2, n1, 2 * F_WIDTH), F32),
        grid=(bsz, n2 // g),
        in_specs=[pl.BlockSpec((1, n1, g, 2 * F_WIDTH), lambda b, s: (b, 0, s, 0)),
                  pl.BlockSpec((g, n1, 2 * n1), lambda b, s: (s, 0, 0))],
        out_specs=pl.BlockSpec((1, g, n1, 2 * F_WIDTH), lambda b, s: (b, s, 0, 0)),
        compiler_params=_cparams(("parallel", "parallel")),
        name="fft_a",
    )(wc.reshape(bsz, n1, n2, 2 * F_WIDTH), ta)
    y = pl.pallas_call(
        _fft_c_kernel,
        out_shape=jax.ShapeDtypeStruct((bsz, n2, n1, F_WIDTH), F32),
        grid=(bsz, n1 // g),
        in_specs=[pl.BlockSpec((1, n2, g, 2 * F_WIDTH), lambda b, k: (b, 0, k, 0)),
                  pl.BlockSpec((n2, 2 * n2), lambda b, k: (0, 0))],
        out_specs=pl.BlockSpec((1, n2, g, F_WIDTH), lambda b, k: (b, 0, k, 0)),
        compiler_params=_cparams(("parallel", "parallel")),
        name="fft_c",
    )(x1, tc)
    return y.reshape(bsz * seq, F_WIDTH)


def _dft_tables(seq):
    n2 = FFT_N2
    n1 = seq // n2
    kc = np.arange(F_GROUP_DIM)
    ang = 2.0 * np.pi * ((kc[:, None] * kc[None, :]) % F_GROUP_DIM) / F_GROUP_DIM
    sc = 1.0 / math.sqrt(F_GROUP_DIM)
    fc = np.concatenate([np.cos(ang) * sc, -np.sin(ang) * sc], axis=1)
    k1 = np.arange(n1)[None, :, None]
    s1 = np.arange(n1)[None, None, :]
    s2 = np.arange(n2)[:, None, None]
    ang = 2.0 * np.pi * ((k1 * (s2 + n2 * s1)) % seq) / seq
    sa = 1.0 / math.sqrt(seq)
    ta = np.concatenate([np.cos(ang) * sa, np.sin(ang) * sa], axis=2)
    k2 = np.arange(n2)
    ang = 2.0 * np.pi * ((k2[:, None] * k2[None, :]) % n2) / n2
    tc = np.concatenate([np.cos(ang), np.sin(ang)], axis=1)
    as_bf16 = lambda a: jnp.asarray(a.astype(np.float32)).astype(BF16)
    return as_bf16(fc), as_bf16(ta), as_bf16(tc)


def _gla_levels(c):
    halves = []
    s = c // 2
    while s >= 1:
        halves.append(s)
        s //= 2
    return halves


def _gla_tables(c):
    r = np.arange(c)[:, None]
    l = np.arange(c)[None, :]
    mats = [(l <= r), (l > r)]
    masks = []
    for s in _gla_levels(c):
        m = (r // (2 * s)) * (2 * s) + s - 1
        q_side = r > m
        mats.append(np.where(q_side, (l > m) & (l <= r), (l > r) & (l <= m)))
        i, j = r, l
        masks.append((i // (2 * s) == j // (2 * s)) & (i % (2 * s) >= s) & (j % (2 * s) < s))
    masks.append(r == l)
    mt = np.stack(mats).astype(np.float32)
    mk = np.stack(masks).astype(np.float32)
    mt = np.stack([mt, mt[:, ::-1, ::-1]])
    mk = np.stack([mk, mk[:, ::-1, ::-1]])
    nt = mt.shape[1]
    return (jnp.asarray(mt.reshape(2, nt * c, c)).astype(BF16), jnp.asarray(mk))


def _gla_kernel(qf_ref, kf_ref, vf_ref, af_ref, qb_ref, kb_ref, vb_ref, ab_ref,
                w2h_ref, w2l_ref, ba_ref, mt_ref, mk_ref, of_ref, ob_ref, st_ref, *, chunk, n_lvl):
    c = chunk
    dirs = range(N_DIRS)
    heads = range(GLA_HEADS)
    dk = lambda h: slice(h * GLA_DK, (h + 1) * GLA_DK)
    dv = lambda h: slice(h * GLA_DV, (h + 1) * GLA_DV)
    q_refs, k_refs, v_refs = (qf_ref, qb_ref), (kf_ref, kb_ref), (vf_ref, vb_ref)
    a_refs, o_refs = (af_ref, ab_ref), (of_ref, ob_ref)

    @pl.when(pl.program_id(1) == 0)
    def _():
        st_ref[...] = jnp.zeros_like(st_ref)

    ex = []
    for d in dirs:
        a1h, a1l = _split_bf16(a_refs[d][...])
        w2h = w2h_ref[d]
        z = _dot(a1h, w2h) + _dot(a1l, w2h) + _dot(a1h, w2l_ref[d]) + ba_ref[d]
        la = jax.nn.log_sigmoid(z) * (1.0 / GATE_TAU)
        lah, lal = _split_bf16(la)
        e = _dot(mt_ref[d], jnp.concatenate([lah, lal], axis=1))
        ex.append(jnp.exp(e[:, :GLA_QK] + e[:, GLA_QK:]))

    qb = [q_refs[d][...] for d in dirs]
    kb = [k_refs[d][...] for d in dirs]
    q = [x.astype(F32) for x in qb]
    k = [x.astype(F32) for x in kb]
    attn = [[lax.dot_general(qb[d][:, dk(h)], kb[d][:, dk(h)], _NT, preferred_element_type=F32)
             * mk_ref[d, n_lvl] for h in heads] for d in dirs]
    for lvl in range(n_lvl):
        for d in dirs:
            e = ex[d][(2 + lvl) * c:(3 + lvl) * c]
            qe = (q[d] * e).astype(BF16)
            ke = (k[d] * e).astype(BF16)
            for h in heads:
                a = lax.dot_general(qe[:, dk(h)], ke[:, dk(h)], _NT, preferred_element_type=F32)
                attn[d][h] = attn[d][h] + a * mk_ref[d, lvl]
    for d in dirs:
        e_cum = ex[d][0:c]
        e_rev = ex[d][c:2 * c]
        qc = (q[d] * e_cum).astype(BF16)
        kd = (k[d] * e_rev).astype(BF16)
        tot = e_cum[0:1] * e_rev[0:1]
        for h in heads:
            v = v_refs[d][:, dv(h)]
            st = st_ref[d, h]
            o = _dot(attn[d][h].astype(BF16), v)
            o = o + lax.dot_general(qc[:, dk(h)], st.astype(BF16), _NT, preferred_element_type=F32)
            o_refs[d][:, dv(h)] = o.astype(BF16)
            st_ref[d, h] = st * tot[:, dk(h)] + lax.dot_general(v, kd[:, dk(h)], _TN,
                                                                preferred_element_type=F32)


def _gla(p, a1, w2h, w2l, ba, mt, mk, bsz, seq):
    t = p.shape[0]
    c = min(GLA_CHUNK, seq)
    nc = seq // c
    n_lvl = len(_gla_levels(c))
    fwd = lambda b, c_: b * nc + c_
    bwd = lambda b, c_: b * nc + nc - 1 - c_
    qoff = F_WIDTH // GLA_QK
    koff = (F_WIDTH + GLA_QK) // GLA_QK
    voff = (F_WIDTH + 2 * GLA_QK) // GLA_V
    whole = lambda a: pl.BlockSpec(a.shape, lambda b, c_: (0,) * a.ndim)
    streams = lambda row: [pl.BlockSpec((c, GLA_QK), lambda b, c_: (row(b, c_), qoff)),
                           pl.BlockSpec((c, GLA_QK), lambda b, c_: (row(b, c_), koff)),
                           pl.BlockSpec((c, GLA_V), lambda b, c_: (row(b, c_), voff)),
                           pl.BlockSpec((c, LANES), lambda b, c_: (row(b, c_), 0))]
    return pl.pallas_call(
        functools.partial(_gla_kernel, chunk=c, n_lvl=n_lvl),
        out_shape=(jax.ShapeDtypeStruct((t, GLA_V), BF16), jax.ShapeDtypeStruct((t, GLA_V), BF16)),
        grid=(bsz, nc),
        in_specs=streams(fwd) + streams(bwd) + [whole(w2h), whole(w2l), whole(ba), whole(mt), whole(mk)],
        out_specs=(pl.BlockSpec((c, GLA_V), lambda b, c_: (fwd(b, c_), 0)),
                   pl.BlockSpec((c, GLA_V), lambda b, c_: (bwd(b, c_), 0))),
        scratch_shapes=[pltpu.VMEM((N_DIRS, GLA_HEADS, GLA_DV, GLA_DK), F32)],
        compiler_params=_cparams(("parallel", "arbitrary")),
        name="gla",
    )(p, p, p, a1, p, p, p, a1, w2h, w2l, ba, mt, mk)


def _mix_kernel(yf_ref, of_ref, ob_ref, r_ref, m_ref, x_ref, gn_ref, wf_ref, wg_ref, wo_ref,
                g1_ref, n2g_ref, sc2_ref, sh2_ref, h_ref, n2t_ref):
    o = of_ref[...].astype(F32) + ob_ref[...].astype(F32)
    parts = []
    for h in range(GLA_HEADS):
        oh = o[:, h * GLA_DV:(h + 1) * GLA_DV]
        ms = jnp.mean(oh * oh, axis=-1, keepdims=True)
        parts.append(oh * lax.rsqrt(ms + EPS * GLA_DK) * gn_ref[...])
    og = jnp.concatenate(parts, axis=1) * jax.nn.silu(r_ref[...].astype(F32))
    yg = _dot(og.astype(BF16), wg_ref[...])
    yf = _dot(yf_ref[...].astype(BF16), wf_ref[...])
    d = yf.shape[1]
    m = m_ref[...].astype(F32)
    mixed = jax.nn.sigmoid(m[:, :d]) * yf + jax.nn.sigmoid(m[:, d:]) * yg
    h1 = x_ref[...] + g1_ref[0] * _dot(mixed.astype(BF16), wo_ref[...])
    h_ref[...] = h1
    ms = jnp.mean(h1 * h1, axis=-1, keepdims=True)
    n2 = h1 * lax.rsqrt(ms + EPS) * n2g_ref[...]
    n2t_ref[...] = (n2 * (1.0 + sc2_ref[0]) + sh2_ref[0]).T.astype(BF16)


def _mix(yf, o_f, o_b, p, x2, gn, wf, wg, wo, mod3, n2g, seq, tm=256):
    t, d = x2.shape
    tm = min(tm, seq)
    bidx = lambda i: (i * tm) // seq
    const = lambda shape: pl.BlockSpec(shape, lambda i: (0,) * len(shape))
    roff = (F_WIDTH + 2 * GLA_QK + GLA_V) // GLA_V
    moff = (F_WIDTH + 2 * GLA_QK + 2 * GLA_V) // (2 * d)
    return pl.pallas_call(
        _mix_kernel,
        out_shape=(jax.ShapeDtypeStruct((t, d), F32), jax.ShapeDtypeStruct((d, t), BF16)),
        grid=(t // tm,),
        in_specs=[pl.BlockSpec((tm, F_WIDTH), lambda i: (i, 0)),
                  pl.BlockSpec((tm, GLA_V), lambda i: (i, 0)),
                  pl.BlockSpec((tm, GLA_V), lambda i: (i, 0)),
                  pl.BlockSpec((tm, GLA_V), lambda i: (i, roff)),
                  pl.BlockSpec((tm, 2 * d), lambda i: (i, moff)),
                  pl.BlockSpec((tm, d), lambda i: (i, 0)),
                  const((1, GLA_DV)),
                  const((F_WIDTH, d)), const((GLA_V, d)), const((d, d)),
                  pl.BlockSpec((1, 1, d), lambda i: (bidx(i) * N_MOD + MOD_GATE1, 0, 0)),
                  const((1, d)),
                  pl.BlockSpec((1, 1, d), lambda i: (bidx(i) * N_MOD + MOD_SCALE2, 0, 0)),
                  pl.BlockSpec((1, 1, d), lambda i: (bidx(i) * N_MOD + MOD_SHIFT2, 0, 0))],
        out_specs=(pl.BlockSpec((tm, d), lambda i: (i, 0)), pl.BlockSpec((d, tm), lambda i: (0, i))),
        compiler_params=_cparams(("parallel",)),
        name="mix",
    )(yf, o_f, o_b, p, p, x2, gn.reshape(1, GLA_DV), wf, wg, wo, mod3, n2g.reshape(1, d), mod3, mod3)


def _sorting_network(n):
    def merge(lo, hi, r):
        step = r * 2
        if step < hi - lo:
            yield from merge(lo, hi, step)
            yield from merge(lo + r, hi, step)
            yield from ((i, i + r) for i in range(lo + r, hi - r, step))
        else:
            yield (lo, lo + r)

    def sort(lo, hi):
        if hi - lo >= 1:
            mid = lo + (hi - lo) // 2
            yield from sort(lo, mid)
            yield from sort(mid + 1, hi)
            yield from merge(lo, hi, 1)

    return list(sort(0, n - 1))


def _top_values(s, n):
    sub = s.shape[0] // n
    vs = [s[i * sub:(i + 1) * sub] for i in range(n)]
    for i, j in _sorting_network(n):
        vs[i], vs[j] = jnp.maximum(vs[i], vs[j]), jnp.minimum(vs[i], vs[j])
    rows = []
    for r in range(n):
        mx = jnp.max(vs[0], axis=0, keepdims=True)
        rows.append(mx)
        pop = vs[0] == mx
        for i in range(n - r - 1):
            vs[i] = jnp.where(pop, vs[i + 1], vs[i])
    return jnp.concatenate(rows, axis=0)


def _pair_candidates(e1, e2):
    rows = [e1[0:1] * e2]
    for a in range(1, 8):
        rows.append(e1[a:a + 1] * e2[0:8])
    rows.append(e1[8:16] * e2[0:1])
    return jnp.concatenate(rows, axis=0)


def _kth_largest_product(e1, e2, k):
    half = k // 2
    head = e1 * e2[0:1]
    deep = [e1[:half] * e2[b:b + 1] for b in range(1, k)]
    rem = jnp.full_like(head[0:1], float(k))
    theta = jnp.zeros_like(rem)
    for r in range(k):
        mx = jnp.max(head, axis=0, keepdims=True)
        pop = head == mx
        theta = jnp.where(rem > 0.0, mx, theta)
        rem = rem - jnp.sum(jnp.where(pop, 1.0, 0.0), axis=0, keepdims=True)
        if r + 1 < k:
            head = jnp.concatenate([jnp.where(pop[:half], deep[0], head[:half]),
                                    jnp.where(pop[half:], -1.0, head[half:])], axis=0)
            for i in range(len(deep) - 1 - r):
                deep[i] = jnp.where(pop[:half], deep[i + 1], deep[i])
    return theta


def _peer_sel_kernel(n2t_ref, wqt_ref, keys_ref, e1_ref, e2_ref, th_ref):
    qt = _dot(wqt_ref[...], n2t_ref[...])
    for h in range(PEER_HEADS):
        base = h * 2 * PEER_HALF
        s1 = _dot(keys_ref[h, 0], qt[base:base + PEER_HALF].astype(BF16))
        s2 = _dot(keys_ref[h, 1], qt[base + PEER_HALF:base + 2 * PEER_HALF].astype(BF16))
        v1 = _top_values(s1, PEER_TOPK)
        v2 = _top_values(s2, PEER_TOPK)
        e1 = jnp.exp(v1 - v1[0:1])
        e2 = jnp.exp(v2 - v2[0:1])
        cand = _pair_candidates(e1, e2)
        theta = _kth_largest_product(e1, e2, PEER_TOPK)
        sel = cand >= theta
        rz = 1.0 / jnp.sum(jnp.where(sel, cand, 0.0), axis=0, keepdims=True)
        candn = _pair_candidates(e1 * rz, e2)
        th_ref[h:h + 1, :] = jnp.min(jnp.where(sel, candn, jnp.inf), axis=0, keepdims=True)
        e1_ref[h] = jnp.exp(s1 - v1[0:1]) * rz
        e2_ref[h] = jnp.exp(s2 - v2[0:1])


def _peer_sel(n2t, wqt, keys, tm=256):
    d, t = n2t.shape
    tm = min(tm, t)
    nq = wqt.shape[0]
    return pl.pallas_call(
        _peer_sel_kernel,
        out_shape=(jax.ShapeDtypeStruct((PEER_HEADS, N_KEYS, t), F32),
                   jax.ShapeDtypeStruct((PEER_HEADS, N_KEYS, t), F32),
                   jax.ShapeDtypeStruct((PEER_HEADS, t), F32)),
        grid=(t // tm,),
        in_specs=[pl.BlockSpec((d, tm), lambda i: (0, i)),
                  pl.BlockSpec((nq, d), lambda i: (0, 0)),
                  pl.BlockSpec((PEER_HEADS, 2, N_KEYS, PEER_HALF), lambda i: (0, 0, 0, 0))],
        out_specs=(pl.BlockSpec((PEER_HEADS, N_KEYS, tm), lambda i: (0, 0, i)),
                   pl.BlockSpec((PEER_HEADS, N_KEYS, tm), lambda i: (0, 0, i)),
                   pl.BlockSpec((PEER_HEADS, tm), lambda i: (0, i))),
        compiler_params=_cparams(("parallel",)),
        name="peer_sel",
    )(n2t, wqt, keys)


def _peer_kernel(n2t_ref, e1_ref, e2_ref, th_ref, u_ref, vt_ref, out_ref,
                 a_scr, w_scr, *, n_e, ib, tt):
    k = pl.program_id(0)
    e_out = lax.rem(jnp.maximum(k - 2, 0), n_e)

    @pl.when(k == 0)
    def _():
        a_scr[1] = jnp.zeros(a_scr.shape[1:], F32)
        w_scr[0] = jnp.zeros(w_scr.shape[1:], BF16)

    @pl.when((k == 0) | ((k >= 2) & (e_out == 0)))
    def _():
        out_ref[...] = jnp.zeros_like(out_ref)

    def gate_unit(a_prv, w_prv, ii, lc):
        rows = slice(ii * N_KEYS, (ii + 1) * N_KEYS)
        sl = slice(lc * LANES, (lc + 1) * LANES)
        g = None
        for h in range(PEER_HEADS):
            p = e1_ref[h, ii:ii + 1, sl] * e2_ref[h, :, sl]
            p = jnp.where(p >= th_ref[h:h + 1, sl], p, 0.0)
            g = p if g is None else g + p
        a = a_prv[rows, sl]
        act = 0.5 * a * (1.0 + lax.erf(a * (1.0 / math.sqrt(2.0))))
        w_prv[rows, sl] = (act * g).astype(BF16)

    def stages(a_cur, a_prv, w_cur, w_prv):
        eb, d = a_cur.shape[0], out_ref.shape[0]

        def pre_unit(r, c):
            rows, toks = slice(r * UNIT_ROWS, (r + 1) * UNIT_ROWS), slice(c * MXU_TILE, (c + 1) * MXU_TILE)
            a_cur[rows, toks] = _dot(u_ref[rows, :], n2t_ref[:, toks])

        def val_unit(r, c):
            rows, toks = slice(r * UNIT_ROWS, (r + 1) * UNIT_ROWS), slice(c * MXU_TILE, (c + 1) * MXU_TILE)
            out_ref[rows, toks] += _dot(vt_ref[0, rows, :], w_cur[:, toks])

        n_tc = tt // MXU_TILE
        pre = [functools.partial(pre_unit, r, c) for r in range(eb // UNIT_ROWS) for c in range(n_tc)]
        val = [functools.partial(val_unit, r, c) for r in range(d // UNIT_ROWS) for c in range(n_tc)]
        gate = [functools.partial(gate_unit, a_prv, w_prv, ii, lc) for ii in range(ib) for lc in range(tt // LANES)]
        for unit in _interleave(pre, val, gate):
            unit()

    cur = lax.rem(k, 2)
    prv = 1 - cur
    stages(a_scr.at[cur], a_scr.at[prv], w_scr.at[cur], w_scr.at[prv])


def _block_transpose_kernel(v_ref, o_ref):
    o_ref[0] = v_ref[...].T.astype(BF16)


def _block_transpose(v, rows):
    e, d = v.shape
    return pl.pallas_call(
        _block_transpose_kernel,
        out_shape=jax.ShapeDtypeStruct((e // rows, d, rows), BF16),
        grid=(e // rows,),
        in_specs=[pl.BlockSpec((rows, d), lambda i: (i, 0))],
        out_specs=pl.BlockSpec((1, d, rows), lambda i: (i, 0, 0)),
        compiler_params=_cparams(("parallel",)),
        name="block_transpose",
    )(v)


def _peer(n2t, e1, e2, th, u, v, seq, tt=1024, eb=1024):
    d, t = n2t.shape
    tt = min(tt, seq)
    n_e = u.shape[0] // eb
    vt = _block_transpose(v, eb)
    n_blk = (t // tt) * n_e
    ib = eb // N_KEYS
    s1 = lambda k: jnp.minimum(k, n_blk - 1)
    s2 = lambda k: jnp.clip(k - 1, 0, n_blk - 1)
    s3 = lambda k: jnp.clip(k - 2, 0, n_blk - 1)
    once = pl.Buffered(1)
    return pl.pallas_call(
        functools.partial(_peer_kernel, n_e=n_e, ib=ib, tt=tt),
        out_shape=jax.ShapeDtypeStruct((d, t), F32),
        grid=(n_blk + 2,),
        in_specs=[pl.BlockSpec((d, tt), lambda k: (0, s1(k) // n_e), pipeline_mode=once),
                  pl.BlockSpec((PEER_HEADS, ib, tt), lambda k: (0, s2(k) % n_e, s2(k) // n_e)),
                  pl.BlockSpec((PEER_HEADS, N_KEYS, tt), lambda k: (0, 0, s2(k) // n_e), pipeline_mode=once),
                  pl.BlockSpec((PEER_HEADS, tt), lambda k: (0, s2(k) // n_e)),
                  pl.BlockSpec((eb, d), lambda k: (s1(k) % n_e, 0)),
                  pl.BlockSpec((1, d, eb), lambda k: (s3(k) % n_e, 0, 0))],
        out_specs=pl.BlockSpec((d, tt), lambda k: (0, s3(k) // n_e)),
        scratch_shapes=[pltpu.VMEM((2, eb, tt), F32), pltpu.VMEM((2, eb, tt), BF16)],
        compiler_params=_cparams(("arbitrary",)),
        name="peer",
    )(n2t, e1, e2, th, u, vt)


def _peer_out_kernel(pt_ref, h1_ref, g2_ref, fg_ref, out_ref, *, final_norm):
    h2 = h1_ref[...] + g2_ref[0] * pt_ref[...].T
    if final_norm:
        ms = jnp.mean(h2 * h2, axis=-1, keepdims=True)
        h2 = h2 * lax.rsqrt(ms + EPS) * fg_ref[...]
    out_ref[...] = h2


def _peer_out(pt, h1, mod3, fg, seq, final_norm, tm=512):
    t, d = h1.shape
    tm = min(tm, seq)
    bidx = lambda i: (i * tm) // seq
    return pl.pallas_call(
        functools.partial(_peer_out_kernel, final_norm=final_norm),
        out_shape=jax.ShapeDtypeStruct((t, d), F32),
        grid=(t // tm,),
        in_specs=[pl.BlockSpec((d, tm), lambda i: (0, i)),
                  pl.BlockSpec((tm, d), lambda i: (i, 0)),
                  pl.BlockSpec((1, 1, d), lambda i: (bidx(i) * N_MOD + MOD_GATE2, 0, 0)),
                  pl.BlockSpec((1, d), lambda i: (0, 0))],
        out_specs=pl.BlockSpec((tm, d), lambda i: (i, 0)),
        compiler_params=_cparams(("parallel",)),
        name="peer_out",
    )(pt, h1, mod3, fg.reshape(1, d))


def kernel(x, c, w_ada, b_ada, norm1_g, w_in, w_fnet, gla_w_a2, gla_b_a, gla_norm_g, w_gla, w_out,
           norm2_g, peer_w_q, peer_keys, peer_u, peer_v, final_norm_g):
    bsz, seq, d = x.shape
    t = bsz * seq
    depth = w_ada.shape[0]
    fc, ta, tc = _dft_tables(seq)
    mt, mk = _gla_tables(min(GLA_CHUNK, seq))
    n_main = F_WIDTH + 2 * GLA_QK + 2 * GLA_V
    h = x.reshape(t, d)
    for l in range(depth):
        mod3 = _ada(c, w_ada[l], b_ada[l]).reshape(bsz * N_MOD, 1, d)
        w_main = jnp.concatenate([w_in[l][:, :n_main], w_in[l][:, n_main + N_DIRS * GATE_RANK:]], axis=1)
        w_a1 = jnp.zeros((d, LANES), F32).at[:, :N_DIRS * GATE_RANK].set(
            w_in[l][:, n_main:n_main + N_DIRS * GATE_RANK])
        p, a1 = _inproj(h, norm1_g[l], mod3, w_main.astype(BF16), w_a1.astype(BF16), seq)

        yf = _seq_dft(_chan_dft(p, fc), bsz, seq, ta, tc)

        w2 = jnp.zeros((N_DIRS, LANES, GLA_QK), F32)
        for dr in range(N_DIRS):
            w2 = w2.at[dr, dr * GATE_RANK:(dr + 1) * GATE_RANK].set(gla_w_a2[l][dr])
        w2h = w2.astype(BF16)
        w2l = (w2 - w2h.astype(F32)).astype(BF16)
        ba = gla_b_a[l].reshape(N_DIRS, 1, GLA_QK)
        o_f, o_b = _gla(p, a1, w2h, w2l, ba, mt, mk, bsz, seq)

        h1, n2t = _mix(yf, o_f, o_b, p, h, gla_norm_g[l], w_fnet[l].astype(BF16), w_gla[l].astype(BF16),
                      w_out[l].astype(BF16), mod3, norm2_g[l], seq)

        e1, e2, th = _peer_sel(n2t, peer_w_q[l].T.astype(BF16), peer_keys[l].astype(BF16))
        pt = _peer(n2t, e1, e2, th, peer_u[l].astype(BF16), peer_v[l], seq)
        h = _peer_out(pt, h1, mod3, final_norm_g, seq, final_norm=(l == depth - 1))
    return h.reshape(bsz, seq, d)
```

```python
import functools
import math

import numpy as np
import jax
import jax.numpy as jnp
from jax import lax
from jax.experimental import pallas as pl
from jax.experimental.pallas import tpu as pltpu

BF16 = jnp.bfloat16
F32 = jnp.float32

EPS = 1e-6
F_GROUPS = 4
F_GROUP_DIM = 256
F_WIDTH = 1024
GLA_HEADS = 4
GLA_DK = 128
GLA_DV = 256
GLA_QK = 512
GLA_V = 1024
GATE_RANK = 16
GATE_TAU = 16.0
N_DIRS = 2
PEER_HEADS = 8
PEER_HALF = 128
N_KEYS = 128
PEER_TOPK = 16
MOD_SHIFT1, MOD_SCALE1, MOD_GATE1, MOD_SHIFT2, MOD_SCALE2, MOD_GATE2 = range(6)
N_MOD = 6

LANES = 128
SUBLANES = 8
MXU_TILE = 256
UNIT_ROWS = 1024
GLA_CHUNK = 128
FFT_N2 = 128
FFT_GROUP = 8
VMEM_LIMIT = 56 * 1024 * 1024

_NT = (((1,), (1,)), ((), ()))
_TN = (((0,), (0,)), ((), ()))


def _cparams(sem):
    return pltpu.CompilerParams(dimension_semantics=sem, vmem_limit_bytes=VMEM_LIMIT)


def _dot(a, b):
    return jnp.dot(a, b, preferred_element_type=F32)


def _interleave(*streams):
    keyed = [((i + 0.5) / len(s), si, i, item) for si, s in enumerate(streams) for i, item in enumerate(s)]
    return [item for _, _, _, item in sorted(keyed, key=lambda t: t[:3])]


def _split_bf16(a):
    hi = a.astype(BF16)
    lo = (a - hi.astype(F32)).astype(BF16)
    return hi, lo


def _ada_kernel(c_ref, w_ref, b_ref, o_ref):
    s = jax.nn.silu(c_ref[...])
    o_ref[...] = _dot(s.astype(BF16), w_ref[...].astype(BF16)) + b_ref[...]


def _ada(c, w, b, tn=1536):
    bsz, d = c.shape
    n = w.shape[1]
    cp = jnp.zeros((SUBLANES, d), F32).at[:bsz].set(c)
    out = pl.pallas_call(
        _ada_kernel,
        out_shape=jax.ShapeDtypeStruct((SUBLANES, n), F32),
        grid=(n // tn,),
        in_specs=[pl.BlockSpec((SUBLANES, d), lambda j: (0, 0)),
                  pl.BlockSpec((d, tn), lambda j: (0, j)),
                  pl.BlockSpec((1, tn), lambda j: (0, j))],
        out_specs=pl.BlockSpec((SUBLANES, tn), lambda j: (0, j)),
        compiler_params=_cparams(("arbitrary",)),
        name="ada",
    )(cp, w, b.reshape(1, n))
    return out[:bsz]


def _inproj_kernel(x_ref, g_ref, sc_ref, sh_ref, w_ref, wa_ref, p_ref, a1_ref, n_scr):
    @pl.when(pl.program_id(1) == 0)
    def _():
        x = x_ref[...]
        ms = jnp.mean(x * x, axis=-1, keepdims=True)
        y = x * lax.rsqrt(ms + EPS) * g_ref[...]
        nb = (y * (1.0 + sc_ref[0]) + sh_ref[0]).astype(BF16)
        n_scr[...] = nb
        a1_ref[...] = _dot(nb, wa_ref[...])

    p_ref[...] = _dot(n_scr[...], w_ref[...]).astype(BF16)


def _inproj(x2, g, mod3, w, wa, seq, tm=1024, tn=1024):
    t, d = x2.shape
    n = w.shape[1]
    tm = min(tm, seq)
    bidx = lambda i: (i * tm) // seq
    return pl.pallas_call(
        _inproj_kernel,
        out_shape=(jax.ShapeDtypeStruct((t, n), BF16), jax.ShapeDtypeStruct((t, LANES), F32)),
        grid=(t // tm, n // tn),
        in_specs=[pl.BlockSpec((tm, d), lambda i, j: (i, 0)),
                  pl.BlockSpec((1, d), lambda i, j: (0, 0)),
                  pl.BlockSpec((1, 1, d), lambda i, j: (bidx(i) * N_MOD + MOD_SCALE1, 0, 0)),
                  pl.BlockSpec((1, 1, d), lambda i, j: (bidx(i) * N_MOD + MOD_SHIFT1, 0, 0)),
                  pl.BlockSpec((d, tn), lambda i, j: (0, j)),
                  pl.BlockSpec((d, LANES), lambda i, j: (0, 0))],
        out_specs=(pl.BlockSpec((tm, tn), lambda i, j: (i, j)),
                   pl.BlockSpec((tm, LANES), lambda i, j: (i, 0))),
        scratch_shapes=[pltpu.VMEM((tm, d), BF16)],
        compiler_params=_cparams(("parallel", "arbitrary")),
        name="inproj",
    )(x2, g.reshape(1, d), mod3, mod3, w, wa)


def _chan_dft_kernel(z_ref, f_ref, w_ref):
    for g in range(F_GROUPS):
        lo, hi = g * F_GROUP_DIM, (g + 1) * F_GROUP_DIM
        r = _dot(z_ref[:, lo:hi], f_ref[...])
        w_ref[:, lo:hi] = r[:, :F_GROUP_DIM]
        w_ref[:, F_WIDTH + lo:F_WIDTH + hi] = r[:, F_GROUP_DIM:]


def _chan_dft(p, fc, tm=1024):
    t = p.shape[0]
    tm = min(tm, t)
    return pl.pallas_call(
        _chan_dft_kernel,
        out_shape=jax.ShapeDtypeStruct((t, 2 * F_WIDTH), F32),
        grid=(t // tm,),
        in_specs=[pl.BlockSpec((tm, F_WIDTH), lambda i: (i, 0)),
                  pl.BlockSpec((F_GROUP_DIM, 2 * F_GROUP_DIM), lambda i: (0, 0))],
        out_specs=pl.BlockSpec((tm, 2 * F_WIDTH), lambda i: (i, 0)),
        compiler_params=_cparams(("parallel",)),
        name="chan_dft",
    )(p, fc)


def _fft_a_kernel(w_ref, t_ref, x_ref):
    for j in range(FFT_GROUP):
        w = w_ref[0, :, j, :]
        rot = jnp.concatenate([w[:, F_WIDTH:], -w[:, :F_WIDTH]], axis=1)
        rhs = jnp.concatenate([w, rot], axis=0).astype(BF16)
        x_ref[0, j] = _dot(t_ref[j], rhs)


def _fft_c_kernel(x_ref, t_ref, y_ref):
    for j in range(FFT_GROUP):
        x = x_ref[0, :, j, :]
        rhs = jnp.concatenate([x[:, :F_WIDTH], x[:, F_WIDTH:]], axis=0).astype(BF16)
        y_ref[0, :, j, :] = _dot(t_ref[...], rhs)


def _seq_dft(wc, bsz, seq, ta, tc):
    n2 = FFT_N2
    n1 = seq // n2
    g = FFT_GROUP
    x1 = pl.pallas_call(
        _fft_a_kernel,
        out_shape=jax.ShapeDtypeStruct((bsz, n2, n1, 2 * F_WIDTH), F32),
        grid=(bsz, n2 // g),
        in_specs=[pl.BlockSpec((1, n1, g, 2 * F_WIDTH), lambda b, s: (b, 0, s, 0)),
                  pl.BlockSpec((g, n1, 2 * n1), lambda b, s: (s, 0, 0))],
        out_specs=pl.BlockSpec((1, g, n1, 2 * F_WIDTH), lambda b, s: (b, s, 0, 0)),
        compiler_params=_cparams(("parallel", "parallel")),
        name="fft_a",
    )(wc.reshape(bsz, n1, n2, 2 * F_WIDTH), ta)
    y = pl.pallas_call(
        _fft_c_kernel,
        out_shape=jax.ShapeDtypeStruct((bsz, n2, n1, F_WIDTH), F32),
        grid=(bsz, n1 // g),
        in_specs=[pl.BlockSpec((1, n2, g, 2 * F_WIDTH), lambda b, k: (b, 0, k, 0)),
                  pl.BlockSpec((n2, 2 * n2), lambda b, k: (0, 0))],
        out_specs=pl.BlockSpec((1, n2, g, F_WIDTH), lambda b, k: (b, 0, k, 0)),
        compiler_params=_cparams(("parallel", "parallel")),
        name="fft_c",
    )(x1, tc)
    return y.reshape(bsz * seq, F_WIDTH)


def _dft_tables(seq):
    n2 = FFT_N2
    n1 = seq // n2
    kc = np.arange(F_GROUP_DIM)
    ang = 2.0 * np.pi * ((kc[:, None] * kc[None, :]) % F_GROUP_DIM) / F_GROUP_DIM
    sc = 1.0 / math.sqrt(F_GROUP_DIM)
    fc = np.concatenate([np.cos(ang) * sc, -np.sin(ang) * sc], axis=1)
    k1 = np.arange(n1)[None, :, None]
    s1 = np.arange(n1)[None, None, :]
    s2 = np.arange(n2)[:, None, None]
    ang = 2.0 * np.pi * ((k1 * (s2 + n2 * s1)) % seq) / seq
    sa = 1.0 / math.sqrt(seq)
    ta = np.concatenate([np.cos(ang) * sa, np.sin(ang) * sa], axis=2)
    k2 = np.arange(n2)
    ang = 2.0 * np.pi * ((k2[:, None] * k2[None, :]) % n2) / n2
    tc = np.concatenate([np.cos(ang), np.sin(ang)], axis=1)
    as_bf16 = lambda a: jnp.asarray(a.astype(np.float32)).astype(BF16)
    return as_bf16(fc), as_bf16(ta), as_bf16(tc)


def _gla_levels(c):
    halves = []
    s = c // 2
    while s >= 1:
        halves.append(s)
        s //= 2
    return halves


def _gla_tables(c):
    r = np.arange(c)[:, None]
    l = np.arange(c)[None, :]
    mats = [(l <= r), (l > r)]
    masks = []
    for s in _gla_levels(c):
        m = (r // (2 * s)) * (2 * s) + s - 1
        q_side = r > m
        mats.append(np.where(q_side, (l > m) & (l <= r), (l > r) & (l <= m)))
        i, j = r, l
        masks.append((i // (2 * s) == j // (2 * s)) & (i % (2 * s) >= s) & (j % (2 * s) < s))
    masks.append(r == l)
    mt = np.stack(mats).astype(np.float32)
    mk = np.stack(masks).astype(np.float32)
    mt = np.stack([mt, mt[:, ::-1, ::-1]])
    mk = np.stack([mk, mk[:, ::-1, ::-1]])
    nt = mt.shape[1]
    return (jnp.asarray(mt.reshape(2, nt * c, c)).astype(BF16), jnp.asarray(mk))


def _gla_kernel(qf_ref, kf_ref, vf_ref, af_ref, qb_ref, kb_ref, vb_ref, ab_ref,
                w2h_ref, w2l_ref, ba_ref, mt_ref, mk_ref, of_ref, ob_ref, st_ref, *, chunk, n_lvl):
    c = chunk
    dirs = range(N_DIRS)
    heads = range(GLA_HEADS)
    dk = lambda h: slice(h * GLA_DK, (h + 1) * GLA_DK)
    dv = lambda h: slice(h * GLA_DV, (h + 1) * GLA_DV)
    q_refs, k_refs, v_refs = (qf_ref, qb_ref), (kf_ref, kb_ref), (vf_ref, vb_ref)
    a_refs, o_refs = (af_ref, ab_ref), (of_ref, ob_ref)

    @pl.when(pl.program_id(1) == 0)
    def _():
        st_ref[...] = jnp.zeros_like(st_ref)

    ex = []
    for d in dirs:
        a1h, a1l = _split_bf16(a_refs[d][...])
        w2h = w2h_ref[d]
        z = _dot(a1h, w2h) + _dot(a1l, w2h) + _dot(a1h, w2l_ref[d]) + ba_ref[d]
        la = jax.nn.log_sigmoid(z) * (1.0 / GATE_TAU)
        lah, lal = _split_bf16(la)
        e = _dot(mt_ref[d], jnp.concatenate([lah, lal], axis=1))
        ex.append(jnp.exp(e[:, :GLA_QK] + e[:, GLA_QK:]))

    qb = [q_refs[d][...] for d in dirs]
    kb = [k_refs[d][...] for d in dirs]
    q = [x.astype(F32) for x in qb]
    k = [x.astype(F32) for x in kb]
    attn = [[lax.dot_general(qb[d][:, dk(h)], kb[d][:, dk(h)], _NT, preferred_element_type=F32)
             * mk_ref[d, n_lvl] for h in heads] for d in dirs]
    for lvl in range(n_lvl):
        for d in dirs:
            e = ex[d][(2 + lvl) * c:(3 + lvl) * c]
            qe = (q[d] * e).astype(BF16)
            ke = (k[d] * e).astype(BF16)
            for h in heads:
                a = lax.dot_general(qe[:, dk(h)], ke[:, dk(h)], _NT, preferred_element_type=F32)
                attn[d][h] = attn[d][h] + a * mk_ref[d, lvl]
    for d in dirs:
        e_cum = ex[d][0:c]
        e_rev = ex[d][c:2 * c]
        qc = (q[d] * e_cum).astype(BF16)
        kd = (k[d] * e_rev).astype(BF16)
        tot = e_cum[0:1] * e_rev[0:1]
        for h in heads:
            v = v_refs[d][:, dv(h)]
            st = st_ref[d, h]
            o = _dot(attn[d][h].astype(BF16), v)
            o = o + lax.dot_general(qc[:, dk(h)], st.astype(BF16), _NT, preferred_element_type=F32)
            o_refs[d][:, dv(h)] = o.astype(BF16)
            st_ref[d, h] = st * tot[:, dk(h)] + lax.dot_general(v, kd[:, dk(h)], _TN,
                                                                preferred_element_type=F32)


def _gla(p, a1, w2h, w2l, ba, mt, mk, bsz, seq):
    t = p.shape[0]
    c = min(GLA_CHUNK, seq)
    nc = seq // c
    n_lvl = len(_gla_levels(c))
    fwd = lambda b, c_: b * nc + c_
    bwd = lambda b, c_: b * nc + nc - 1 - c_
    qoff = F_WIDTH // GLA_QK
    koff = (F_WIDTH + GLA_QK) // GLA_QK
    voff = (F_WIDTH + 2 * GLA_QK) // GLA_V
    whole = lambda a: pl.BlockSpec(a.shape, lambda b, c_: (0,) * a.ndim)
    streams = lambda row: [pl.BlockSpec((c, GLA_QK), lambda b, c_: (row(b, c_), qoff)),
                           pl.BlockSpec((c, GLA_QK), lambda b, c_: (row(b, c_), koff)),
                           pl.BlockSpec((c, GLA_V), lambda b, c_: (row(b, c_), voff)),
                           pl.BlockSpec((c, LANES), lambda b, c_: (row(b, c_), 0))]
    return pl.pallas_call(
        functools.partial(_gla_kernel, chunk=c, n_lvl=n_lvl),
        out_shape=(jax.ShapeDtypeStruct((t, GLA_V), BF16), jax.ShapeDtypeStruct((t, GLA_V), BF16)),
        grid=(bsz, nc),
        in_specs=streams(fwd) + streams(bwd) + [whole(w2h), whole(w2l), whole(ba), whole(mt), whole(mk)],
        out_specs=(pl.BlockSpec((c, GLA_V), lambda b, c_: (fwd(b, c_), 0)),
                   pl.BlockSpec((c, GLA_V), lambda b, c_: (bwd(b, c_), 0))),
        scratch_shapes=[pltpu.VMEM((N_DIRS, GLA_HEADS, GLA_DV, GLA_DK), F32)],
        compiler_params=_cparams(("parallel", "arbitrary")),
        name="gla",
    )(p, p, p, a1, p, p, p, a1, w2h, w2l, ba, mt, mk)


def _mix_kernel(yf_ref, of_ref, ob_ref, r_ref, m_ref, x_ref, gn_ref, wf_ref, wg_ref, wo_ref,
                g1_ref, n2g_ref, sc2_ref, sh2_ref, h_ref, n2t_ref):
    o = of_ref[...].astype(F32) + ob_ref[...].astype(F32)
    parts = []
    for h in range(GLA_HEADS):
        oh = o[:, h * GLA_DV:(h + 1) * GLA_DV]
        ms = jnp.mean(oh * oh, axis=-1, keepdims=True)
        parts.append(oh * lax.rsqrt(ms + EPS * GLA_DK) * gn_ref[...])
    og = jnp.concatenate(parts, axis=1) * jax.nn.silu(r_ref[...].astype(F32))
    yg = _dot(og.astype(BF16), wg_ref[...])
    yf = _dot(yf_ref[...].astype(BF16), wf_ref[...])
    d = yf.shape[1]
    m = m_ref[...].astype(F32)
    mixed = jax.nn.sigmoid(m[:, :d]) * yf + jax.nn.sigmoid(m[:, d:]) * yg
    h1 = x_ref[...] + g1_ref[0] * _dot(mixed.astype(BF16), wo_ref[...])
    h_ref[...] = h1
    ms = jnp.mean(h1 * h1, axis=-1, keepdims=True)
    n2 = h1 * lax.rsqrt(ms + EPS) * n2g_ref[...]
    n2t_ref[...] = (n2 * (1.0 + sc2_ref[0]) + sh2_ref[0]).T.astype(BF16)


def _mix(yf, o_f, o_b, p, x2, gn, wf, wg, wo, mod3, n2g, seq, tm=256):
    t, d = x2.shape
    tm = min(tm, seq)
    bidx = lambda i: (i * tm) // seq
    const = lambda shape: pl.BlockSpec(shape, lambda i: (0,) * len(shape))
    roff = (F_WIDTH + 2 * GLA_QK + GLA_V) // GLA_V
    moff = (F_WIDTH + 2 * GLA_QK + 2 * GLA_V) // (2 * d)
    return pl.pallas_call(
        _mix_kernel,
        out_shape=(jax.ShapeDtypeStruct((t, d), F32), jax.ShapeDtypeStruct((d, t), BF16)),
        grid=(t // tm,),
        in_specs=[pl.BlockSpec((tm, F_WIDTH), lambda i: (i, 0)),
                  pl.BlockSpec((tm, GLA_V), lambda i: (i, 0)),
                  pl.BlockSpec((tm, GLA_V), lambda i: (i, 0)),
                  pl.BlockSpec((tm, GLA_V), lambda i: (i, roff)),
                  pl.BlockSpec((tm, 2 * d), lambda i: (i, moff)),
                  pl.BlockSpec((tm, d), lambda i: (i, 0)),
                  const((1, GLA_DV)),
                  const((F_WIDTH, d)), const((GLA_V, d)), const((d, d)),
                  pl.BlockSpec((1, 1, d), lambda i: (bidx(i) * N_MOD + MOD_GATE1, 0, 0)),
                  const((1, d)),
                  pl.BlockSpec((1, 1, d), lambda i: (bidx(i) * N_MOD + MOD_SCALE2, 0, 0)),
                  pl.BlockSpec((1, 1, d), lambda i: (bidx(i) * N_MOD + MOD_SHIFT2, 0, 0))],
        out_specs=(pl.BlockSpec((tm, d), lambda i: (i, 0)), pl.BlockSpec((d, tm), lambda i: (0, i))),
        compiler_params=_cparams(("parallel",)),
        name="mix",
    )(yf, o_f, o_b, p, p, x2, gn.reshape(1, GLA_DV), wf, wg, wo, mod3, n2g.reshape(1, d), mod3, mod3)


def _sorting_network(n):
    def merge(lo, hi, r):
        step = r * 2
        if step < hi - lo:
            yield from merge(lo, hi, step)
            yield from merge(lo + r, hi, step)
            yield from ((i, i + r) for i in range(lo + r, hi - r, step))
        else:
            yield (lo, lo + r)

    def sort(lo, hi):
        if hi - lo >= 1:
            mid = lo + (hi - lo) // 2
            yield from sort(lo, mid)
            yield from sort(mid + 1, hi)
            yield from merge(lo, hi, 1)

    return list(sort(0, n - 1))


def _top_values(s, n):
    sub = s.shape[0] // n
    vs = [s[i * sub:(i + 1) * sub] for i in range(n)]
    for i, j in _sorting_network(n):
        vs[i], vs[j] = jnp.maximum(vs[i], vs[j]), jnp.minimum(vs[i], vs[j])
    rows = []
    for r in range(n):
        mx = jnp.max(vs[0], axis=0, keepdims=True)
        rows.append(mx)
        pop = vs[0] == mx
        for i in range(n - r - 1):
            vs[i] = jnp.where(pop, vs[i + 1], vs[i])
    return jnp.concatenate(rows, axis=0)


def _pair_candidates(e1, e2):
    rows = [e1[0:1] * e2]
    for a in range(1, 8):
        rows.append(e1[a:a + 1] * e2[0:8])
    rows.append(e1[8:16] * e2[0:1])
    return jnp.concatenate(rows, axis=0)


def _kth_largest_product(e1, e2, k):
    half = k // 2
    head = e1 * e2[0:1]
    deep = [e1[:half] * e2[b:b + 1] for b in range(1, k)]
    rem = jnp.full_like(head[0:1], float(k))
    theta = jnp.zeros_like(rem)
    for r in range(k):
        mx = jnp.max(head, axis=0, keepdims=True)
        pop = head == mx
        theta = jnp.where(rem > 0.0, mx, theta)
        rem = rem - jnp.sum(jnp.where(pop, 1.0, 0.0), axis=0, keepdims=True)
        if r + 1 < k:
            head = jnp.concatenate([jnp.where(pop[:half], deep[0], head[:half]),
                                    jnp.where(pop[half:], -1.0, head[half:])], axis=0)
            for i in range(len(deep) - 1 - r):
                deep[i] = jnp.where(pop[:half], deep[i + 1], deep[i])
    return theta


def _peer_sel_kernel(n2t_ref, wqt_ref, keys_ref, e1_ref, e2_ref, th_ref):
    qt = _dot(wqt_ref[...], n2t_ref[...])
    for h in range(PEER_HEADS):
        base = h * 2 * PEER_HALF
        s1 = _dot(keys_ref[h, 0], qt[base:base + PEER_HALF].astype(BF16))
        s2 = _dot(keys_ref[h, 1], qt[base + PEER_HALF:base + 2 * PEER_HALF].astype(BF16))
        v1 = _top_values(s1, PEER_TOPK)
        v2 = _top_values(s2, PEER_TOPK)
        e1 = jnp.exp(v1 - v1[0:1])
        e2 = jnp.exp(v2 - v2[0:1])
        cand = _pair_candidates(e1, e2)
        theta = _kth_largest_product(e1, e2, PEER_TOPK)
        sel = cand >= theta
        rz = 1.0 / jnp.sum(jnp.where(sel, cand, 0.0), axis=0, keepdims=True)
        candn = _pair_candidates(e1 * rz, e2)
        th_ref[h:h + 1, :] = jnp.min(jnp.where(sel, candn, jnp.inf), axis=0, keepdims=True)
        e1_ref[h] = jnp.exp(s1 - v1[0:1]) * rz
        e2_ref[h] = jnp.exp(s2 - v2[0:1])


def _peer_sel(n2t, wqt, keys, tm=256):
    d, t = n2t.shape
    tm = min(tm, t)
    nq = wqt.shape[0]
    return pl.pallas_call(
        _peer_sel_kernel,
        out_shape=(jax.ShapeDtypeStruct((PEER_HEADS, N_KEYS, t), F32),
                   jax.ShapeDtypeStruct((PEER_HEADS, N_KEYS, t), F32),
                   jax.ShapeDtypeStruct((PEER_HEADS, t), F32)),
        grid=(t // tm,),
        in_specs=[pl.BlockSpec((d, tm), lambda i: (0, i)),
                  pl.BlockSpec((nq, d), lambda i: (0, 0)),
                  pl.BlockSpec((PEER_HEADS, 2, N_KEYS, PEER_HALF), lambda i: (0, 0, 0, 0))],
        out_specs=(pl.BlockSpec((PEER_HEADS, N_KEYS, tm), lambda i: (0, 0, i)),
                   pl.BlockSpec((PEER_HEADS, N_KEYS, tm), lambda i: (0, 0, i)),
                   pl.BlockSpec((PEER_HEADS, tm), lambda i: (0, i))),
        compiler_params=_cparams(("parallel",)),
        name="peer_sel",
    )(n2t, wqt, keys)


def _peer_kernel(n2t_ref, e1_ref, e2_ref, th_ref, u_ref, vt_ref, out_ref,
                 a_scr, w_scr, *, n_e, ib, tt):
    k = pl.program_id(0)
    e_out = lax.rem(jnp.maximum(k - 2, 0), n_e)

    @pl.when(k == 0)
    def _():
        a_scr[1] = jnp.zeros(a_scr.shape[1:], F32)
        w_scr[0] = jnp.zeros(w_scr.shape[1:], BF16)

    @pl.when((k == 0) | ((k >= 2) & (e_out == 0)))
    def _():
        out_ref[...] = jnp.zeros_like(out_ref)

    def gate_unit(a_prv, w_prv, ii, lc):
        rows = slice(ii * N_KEYS, (ii + 1) * N_KEYS)
        sl = slice(lc * LANES, (lc + 1) * LANES)
        g = None
        for h in range(PEER_HEADS):
            p = e1_ref[h, ii:ii + 1, sl] * e2_ref[h, :, sl]
            p = jnp.where(p >= th_ref[h:h + 1, sl], p, 0.0)
            g = p if g is None else g + p
        a = a_prv[rows, sl]
        act = 0.5 * a * (1.0 + lax.erf(a * (1.0 / math.sqrt(2.0))))
        w_prv[rows, sl] = (act * g).astype(BF16)

    def stages(a_cur, a_prv, w_cur, w_prv):
        eb, d = a_cur.shape[0], out_ref.shape[0]

        def pre_unit(r, c):
            rows, toks = slice(r * UNIT_ROWS, (r + 1) * UNIT_ROWS), slice(c * MXU_TILE, (c + 1) * MXU_TILE)
            a_cur[rows, toks] = _dot(u_ref[rows, :], n2t_ref[:, toks])

        def val_unit(r, c):
            rows, toks = slice(r * UNIT_ROWS, (r + 1) * UNIT_ROWS), slice(c * MXU_TILE, (c + 1) * MXU_TILE)
            out_ref[rows, toks] += _dot(vt_ref[0, rows, :], w_cur[:, toks])

        n_tc = tt // MXU_TILE
        pre = [functools.partial(pre_unit, r, c) for r in range(eb // UNIT_ROWS) for c in range(n_tc)]
        val = [functools.partial(val_unit, r, c) for r in range(d // UNIT_ROWS) for c in range(n_tc)]
        gate = [functools.partial(gate_unit, a_prv, w_prv, ii, lc) for ii in range(ib) for lc in range(tt // LANES)]
        for unit in _interleave(pre, val, gate):
            unit()

    cur = lax.rem(k, 2)
    prv = 1 - cur
    stages(a_scr.at[cur], a_scr.at[prv], w_scr.at[cur], w_scr.at[prv])


def _block_transpose_kernel(v_ref, o_ref):
    o_ref[0] = v_ref[...].T.astype(BF16)


def _block_transpose(v, rows):
    e, d = v.shape
    return pl.pallas_call(
        _block_transpose_kernel,
        out_shape=jax.ShapeDtypeStruct((e // rows, d, rows), BF16),
        grid=(e // rows,),
        in_specs=[pl.BlockSpec((rows, d), lambda i: (i, 0))],
        out_specs=pl.BlockSpec((1, d, rows), lambda i: (i, 0, 0)),
        compiler_params=_cparams(("parallel",)),
        name="block_transpose",
    )(v)


def _peer(n2t, e1, e2, th, u, v, seq, tt=1024, eb=1024):
    d, t = n2t.shape
    tt = min(tt, seq)
    n_e = u.shape[0] // eb
    vt = _block_transpose(v, eb)
    n_blk = (t // tt) * n_e
    ib = eb // N_KEYS
    s1 = lambda k: jnp.minimum(k, n_blk - 1)
    s2 = lambda k: jnp.clip(k - 1, 0, n_blk - 1)
    s3 = lambda k: jnp.clip(k - 2, 0, n_blk - 1)
    once = pl.Buffered(1)
    return pl.pallas_call(
        functools.partial(_peer_kernel, n_e=n_e, ib=ib, tt=tt),
        out_shape=jax.ShapeDtypeStruct((d, t), F32),
        grid=(n_blk + 2,),
        in_specs=[pl.BlockSpec((d, tt), lambda k: (0, s1(k) // n_e), pipeline_mode=once),
                  pl.BlockSpec((PEER_HEADS, ib, tt), lambda k: (0, s2(k) % n_e, s2(k) // n_e)),
                  pl.BlockSpec((PEER_HEADS, N_KEYS, tt), lambda k: (0, 0, s2(k) // n_e), pipeline_mode=once),
                  pl.BlockSpec((PEER_HEADS, tt), lambda k: (0, s2(k) // n_e)),
                  pl.BlockSpec((eb, d), lambda k: (s1(k) % n_e, 0)),
                  pl.BlockSpec((1, d, eb), lambda k: (s3(k) % n_e, 0, 0))],
        out_specs=pl.BlockSpec((d, tt), lambda k: (0, s3(k) // n_e)),
        scratch_shapes=[pltpu.VMEM((2, eb, tt), F32), pltpu.VMEM((2, eb, tt), BF16)],
        compiler_params=_cparams(("arbitrary",)),
        name="peer",
    )(n2t, e1, e2, th, u, vt)


def _peer_out_kernel(pt_ref, h1_ref, g2_ref, fg_ref, out_ref, *, final_norm):
    h2 = h1_ref[...] + g2_ref[0] * pt_ref[...].T
    if final_norm:
        ms = jnp.mean(h2 * h2, axis=-1, keepdims=True)
        h2 = h2 * lax.rsqrt(ms + EPS) * fg_ref[...]
    out_ref[...] = h2


def _peer_out(pt, h1, mod3, fg, seq, final_norm, tm=512):
    t, d = h1.shape
    tm = min(tm, seq)
    bidx = lambda i: (i * tm) // seq
    return pl.pallas_call(
        functools.partial(_peer_out_kernel, final_norm=final_norm),
        out_shape=jax.ShapeDtypeStruct((t, d), F32),
        grid=(t // tm,),
        in_specs=[pl.BlockSpec((d, tm), lambda i: (0, i)),
                  pl.BlockSpec((tm, d), lambda i: (i, 0)),
                  pl.BlockSpec((1, 1, d), lambda i: (bidx(i) * N_MOD + MOD_GATE2, 0, 0)),
                  pl.BlockSpec((1, d), lambda i: (0, 0))],
        out_specs=pl.BlockSpec((tm, d), lambda i: (i, 0)),
        compiler_params=_cparams(("parallel",)),
        name="peer_out",
    )(pt, h1, mod3, fg.reshape(1, d))


def kernel(x, c, w_ada, b_ada, norm1_g, w_in, w_fnet, gla_w_a2, gla_b_a, gla_norm_g, w_gla, w_out,
           norm2_g, peer_w_q, peer_keys, peer_u, peer_v, final_norm_g):
    bsz, seq, d = x.shape
    t = bsz * seq
    depth = w_ada.shape[0]
    fc, ta, tc = _dft_tables(seq)
    mt, mk = _gla_tables(min(GLA_CHUNK, seq))
    n_main = F_WIDTH + 2 * GLA_QK + 2 * GLA_V
    h = x.reshape(t, d)
    for l in range(depth):
        mod3 = _ada(c, w_ada[l], b_ada[l]).reshape(bsz * N_MOD, 1, d)
        w_main = jnp.concatenate([w_in[l][:, :n_main], w_in[l][:, n_main + N_DIRS * GATE_RANK:]], axis=1)
        w_a1 = jnp.zeros((d, LANES), F32).at[:, :N_DIRS * GATE_RANK].set(
            w_in[l][:, n_main:n_main + N_DIRS * GATE_RANK])
        p, a1 = _inproj(h, norm1_g[l], mod3, w_main.astype(BF16), w_a1.astype(BF16), seq)

        yf = _seq_dft(_chan_dft(p, fc), bsz, seq, ta, tc)

        w2 = jnp.zeros((N_DIRS, LANES, GLA_QK), F32)
        for dr in range(N_DIRS):
            w2 = w2.at[dr, dr * GATE_RANK:(dr + 1) * GATE_RANK].set(gla_w_a2[l][dr])
        w2h = w2.astype(BF16)
        w2l = (w2 - w2h.astype(F32)).astype(BF16)
        ba = gla_b_a[l].reshape(N_DIRS, 1, GLA_QK)
        o_f, o_b = _gla(p, a1, w2h, w2l, ba, mt, mk, bsz, seq)

        h1, n2t = _mix(yf, o_f, o_b, p, h, gla_norm_g[l], w_fnet[l].astype(BF16), w_gla[l].astype(BF16),
                      w_out[l].astype(BF16), mod3, norm2_g[l], seq)

        e1, e2, th = _peer_sel(n2t, peer_w_q[l].T.astype(BF16), peer_keys[l].astype(BF16))
        pt = _peer(n2t, e1, e2, th, peer_u[l].astype(BF16), peer_v[l], seq)
        h = _peer_out(pt, h1, mod3, final_norm_g, seq, final_norm=(l == depth - 1))
    return h.reshape(bsz, seq, d)
```
